```python
import math
import jax
import jax.numpy as jnp
from jax import lax
import numpy as np

D_MODEL = 1024
BATCH = 2
SEQ = 16384
DEPTH = 4

HY_WIDTH = 512
HY_ORDER = 2
HY_SHORT = 3
HY_EMB = 33
HY_BANDS = (HY_EMB - 1) // 2
HY_FILTER_HIDDEN = 64
HY_INNER_MLPS = 2
HY_FAST_DECAY = 0.3
HY_SLOW_DECAY = 1.5
HY_DECAY_TARGET = 1e-2

SW_HEADS = 8
SW_KV_HEADS = 2
SW_HEAD_DIM = 64
SW_WINDOW = 128
SW_BLOCK = 128
ROPE_THETA = 500000.0
ROPE_DIM = SW_HEAD_DIM // 4

MEM_TOKENS = 256
MEM_HEADS = 4
MEM_HEAD_DIM = 128

N_BRANCH = 3

N_EXPERTS = 32
TOP_K = 4
D_EXPERT = D_MODEL
SWIGLU_ALPHA = 1.702
SWIGLU_LIMIT = 7.0
MOE_BLOCK = 256

LN_EPS = 1e-5
NEG_INF = -1e30
DEEPNORM_ALPHA = (2 * DEPTH) ** 0.25
DEEPNORM_BETA = (8 * DEPTH) ** -0.25

HY_COLS = 3 * HY_WIDTH
Q_COLS = SW_HEADS * SW_HEAD_DIM
KV_COLS = SW_KV_HEADS * SW_HEAD_DIM
MQ_COLS = MEM_HEADS * MEM_HEAD_DIM
GATE_COLS = N_BRANCH * D_MODEL
IN_COLS = HY_COLS + Q_COLS + 2 * KV_COLS + MQ_COLS + GATE_COLS
IN_SPLITS = (HY_COLS, HY_COLS + Q_COLS, HY_COLS + Q_COLS + KV_COLS,
             HY_COLS + Q_COLS + 2 * KV_COLS, HY_COLS + Q_COLS + 2 * KV_COLS + MQ_COLS)

kernel_name = 'hybrid_hyena_swa_memory_moe_encoder'


def layer_norm(x, g, b):
    xf = x.astype(jnp.float32)
    mu = jnp.mean(xf, axis=-1, keepdims=True)
    var = jnp.mean(jnp.square(xf - mu), axis=-1, keepdims=True)
    y = (xf - mu) * lax.rsqrt(var + LN_EPS)
    return (y * g.astype(jnp.float32) + b.astype(jnp.float32)).astype(x.dtype)


def short_conv(z, w, b):
    L = z.shape[1]
    r = HY_SHORT // 2
    zp = jnp.pad(z, ((0, 0), (r, r), (0, 0)))
    out = zp[:, 0:L] * w[0] + b
    for j in range(1, HY_SHORT):
        out = out + zp[:, j:j + L] * w[j]
    return out


def hyena_filters(L, f1_w, f1_b, f2_w, f2_b, f3_w, freq):
    f32 = jnp.float32
    t01 = jnp.linspace(0.0, 1.0, L, dtype=f32)[:, None]
    w = (2.0 * math.pi) * jnp.arange(L, dtype=f32)[:, None] / L
    bands = jnp.linspace(1e-4, HY_BANDS - 1, HY_BANDS, dtype=f32)
    z = jnp.concatenate([t01, jnp.cos(bands * w), -jnp.sin(bands * w)], axis=-1)
    fr = freq.astype(f32)
    h = jnp.sin(fr * (z @ f1_w.astype(f32) + f1_b.astype(f32)))
    for i in range(HY_INNER_MLPS):
        h = jnp.sin(fr * (h @ f2_w[i].astype(f32) + f2_b[i].astype(f32)))
    h = (h @ f3_w.astype(f32)).reshape(L, HY_ORDER, 2, HY_WIDTH)
    max_decay = math.log(HY_DECAY_TARGET) / HY_FAST_DECAY
    min_decay = math.log(HY_DECAY_TARGET) / HY_SLOW_DECAY
    deltas = jnp.linspace(min_decay, max_decay, HY_WIDTH, dtype=f32)
    window = jnp.exp(-t01 * jnp.abs(deltas))
    h = h * window[:, None, None, :]
    h_fwd, h_bwd = h[:, :, 0], h[:, :, 1]
    k = jnp.concatenate([h_fwd, jnp.zeros_like(h_fwd[:1]), h_bwd[1:][::-1]], axis=0)
    return jnp.fft.rfft(k, axis=0)


def bidir_fftconv(u, k_f, skip):
    L = u.shape[1]
    uf = jnp.fft.rfft(u.astype(jnp.float32), n=2 * L, axis=1)
    y = jnp.fft.irfft(uf * k_f[None], n=2 * L, axis=1)[:, :L]
    return (y + u.astype(jnp.float32) * skip.astype(jnp.float32)).astype(u.dtype)


def partial_rope(x, pos):
    half = ROPE_DIM // 2
    inv = jnp.power(jnp.float32(ROPE_THETA), -jnp.arange(half, dtype=jnp.float32) * (2.0 / ROPE_DIM))
    ang = pos.astype(jnp.float32)[:, None] * inv
    cos = jnp.cos(ang)[None, :, None, :]
    sin = jnp.sin(ang)[None, :, None, :]
    xf = x.astype(jnp.float32)
    x1, x2 = xf[..., :half], xf[..., half:ROPE_DIM]
    out = jnp.concatenate([x1 * cos - x2 * sin, x2 * cos + x1 * sin, xf[..., ROPE_DIM:]], axis=-1)
    return out.astype(x.dtype)


def banded_sink_attention(q, k, v, sink):
    B, S, _, hd = q.shape
    nb = S // SW_BLOCK
    G = SW_HEADS // SW_KV_HEADS
    qb = q.reshape(B, nb, SW_BLOCK, SW_KV_HEADS, G, hd)

    def windows(t):
        tp = jnp.pad(t, ((0, 0), (SW_BLOCK, SW_BLOCK), (0, 0), (0, 0)))
        tp = tp.reshape(B, nb + 2, SW_BLOCK, SW_KV_HEADS, hd)
        return jnp.concatenate([tp[:, :-2], tp[:, 1:-1], tp[:, 2:]], axis=2)

    kw, vw = windows(k), windows(v)
    s = jnp.einsum('bnqhgd,bnkhd->bnhgqk', qb, kw, preferred_element_type=jnp.float32) * (hd ** -0.5)
    blk = jnp.arange(nb)[:, None, None]
    qpos = blk * SW_BLOCK + jnp.arange(SW_BLOCK)[None, :, None]
    kpos = (blk - 1) * SW_BLOCK + jnp.arange(3 * SW_BLOCK)[None, None, :]
    valid = (jnp.abs(qpos - kpos) <= SW_WINDOW) & (kpos >= 0) & (kpos < S)
    s = jnp.where(valid[None, :, None, None], s, NEG_INF)
    sk = sink.astype(jnp.float32).reshape(1, 1, SW_KV_HEADS, G, 1, 1)
    m = jnp.maximum(jnp.max(s, axis=-1, keepdims=True), sk)
    p = jnp.exp(s - m)
    denom = jnp.sum(p, axis=-1, keepdims=True) + jnp.exp(sk - m)
    o = jnp.einsum('bnhgqk,bnkhd->bnqhgd', (p / denom).astype(v.dtype), vw)
    return o.reshape(B, S, SW_HEADS * hd)


def memory_attention(q, mk, mv):
    B, S, H, hd = q.shape
    s = jnp.einsum('bshd,bmhd->bhsm', q, mk, preferred_element_type=jnp.float32) * (hd ** -0.5)
    p = jax.nn.softmax(s, axis=-1).astype(mv.dtype)
    o = jnp.einsum('bhsm,bmhd->bshd', p, mv)
    return o.reshape(B, S, H * hd)


def clamped_swiglu(h):
    h_glu, h_lin = h[..., ::2], h[..., 1::2]
    h_glu = jnp.minimum(h_glu, SWIGLU_LIMIT)
    h_lin = jnp.clip(h_lin, -SWIGLU_LIMIT, SWIGLU_LIMIT)
    return h_glu * jax.nn.sigmoid(SWIGLU_ALPHA * h_glu) * (h_lin + 1.0)


def moe_ffn(x2, router_w, router_b, w1, b1, w2, b2):
    N, D = x2.shape
    logits = (x2 @ router_w + router_b).astype(jnp.float32)
    top_v, top_i = lax.top_k(logits, TOP_K)
    gate = jax.nn.softmax(top_v, axis=-1)
    A = N * TOP_K
    n_blocks = -(-(A + N_EXPERTS * (MOE_BLOCK - 1)) // MOE_BLOCK)
    P = n_blocks * MOE_BLOCK
    e_flat = top_i.reshape(-1).astype(jnp.int32)
    order = jnp.argsort(e_flat)
    e_sorted = e_flat[order]
    tok_sorted = (order // TOP_K).astype(jnp.int32)
    w_sorted = gate.reshape(-1)[order].astype(x2.dtype)
    counts = jnp.bincount(e_flat, length=N_EXPERTS)
    padded = (counts + MOE_BLOCK - 1) // MOE_BLOCK * MOE_BLOCK
    pad_end = jnp.cumsum(padded)
    start = jnp.cumsum(counts) - counts
    dest = pad_end[e_sorted] - padded[e_sorted] + jnp.arange(A, dtype=jnp.int32) - start[e_sorted]
    slot_tok = jnp.full((P,), N, jnp.int32).at[dest].set(tok_sorted)
    slot_w = jnp.zeros((P,), x2.dtype).at[dest].set(w_sorted)
    block_expert = jnp.minimum(
        jnp.searchsorted(pad_end, jnp.arange(n_blocks, dtype=pad_end.dtype) * MOE_BLOCK, side='right'),
        N_EXPERTS - 1)
    x_pad = jnp.concatenate([x2, jnp.zeros((1, D), x2.dtype)], axis=0)

    def expert_block(args):
        tok, wt, e = args
        h = x_pad[tok] @ w1[e] + b1[e]
        y = clamped_swiglu(h) @ w2[e] + b2[e]
        return y * wt[:, None]

    ys = lax.map(expert_block, (slot_tok.reshape(n_blocks, MOE_BLOCK),
                                slot_w.reshape(n_blocks, MOE_BLOCK), block_expert))
    out = jax.ops.segment_sum(ys.reshape(P, D), slot_tok, num_segments=N + 1)
    return out[:N]


def mixer_block(x, mem, pos, w_in, b_gate, hy_conv_w, hy_conv_b, hy_f1_w, hy_f1_b, hy_f2_w, hy_f2_b,
                hy_f3_w, hy_freq, hy_skip, sw_sink, mem_w_kv, w_br_hy, w_br_swa, w_br_mem, w_out):
    B, S, D = x.shape
    proj = x @ w_in
    z_hy, q, k, v, mq, gl = jnp.split(proj, IN_SPLITS, axis=-1)

    z_hy = short_conv(z_hy, hy_conv_w, hy_conv_b)
    hv, hx1, hx2 = jnp.split(z_hy, 3, axis=-1)
    kf = hyena_filters(S, hy_f1_w, hy_f1_b, hy_f2_w, hy_f2_b, hy_f3_w, hy_freq)
    y_hy = hv
    for o, gx in enumerate((hx1, hx2)):
        y_hy = gx * bidir_fftconv(y_hy, kf[:, o], hy_skip[o])

    q = partial_rope(q.reshape(B, S, SW_HEADS, SW_HEAD_DIM), pos)
    k = partial_rope(k.reshape(B, S, SW_KV_HEADS, SW_HEAD_DIM), pos)
    v = v.reshape(B, S, SW_KV_HEADS, SW_HEAD_DIM)
    y_sw = banded_sink_attention(q, k, v, sw_sink)

    M = mem.shape[1]
    mk, mv = jnp.split(mem @ mem_w_kv, 2, axis=-1)
    y_mem = memory_attention(mq.reshape(B, S, MEM_HEADS, MEM_HEAD_DIM),
                             mk.reshape(B, M, MEM_HEADS, MEM_HEAD_DIM),
                             mv.reshape(B, M, MEM_HEADS, MEM_HEAD_DIM))

    g = jax.nn.sigmoid(gl.reshape(B, S, N_BRANCH, D) + b_gate)
    merged = (g[:, :, 0] * (y_hy @ w_br_hy) + g[:, :, 1] * (y_sw @ w_br_swa)
              + g[:, :, 2] * (y_mem @ w_br_mem))
    return merged @ w_out


def setup_inputs(seed: int = 0) -> dict:
    key = jax.random.key(seed)
    ks = iter(jax.random.split(key, 32))

    def nrm(shape, scale):
        return jax.random.normal(next(ks), shape, jnp.float32) * scale

    D, L = D_MODEL, DEPTH
    return {
        'x': nrm((BATCH, SEQ, D), 1.0),
        'mem': nrm((BATCH, MEM_TOKENS, D), 1.0),
        'ln_in_g': 1.0 + nrm((D,), 0.02),
        'ln_in_b': nrm((D,), 0.02),
        'w_in': nrm((L, D, IN_COLS), D ** -0.5),
        'b_gate': nrm((L, N_BRANCH, D), 0.1),
        'hy_conv_w': nrm((L, HY_SHORT, HY_COLS), HY_SHORT ** -0.5),
        'hy_conv_b': nrm((L, HY_COLS), 0.02),
        'hy_f1_w': nrm((L, HY_EMB, HY_FILTER_HIDDEN), HY_EMB ** -0.5),
        'hy_f1_b': nrm((L, HY_FILTER_HIDDEN), 0.5),
        'hy_f2_w': nrm((L, HY_INNER_MLPS, HY_FILTER_HIDDEN, HY_FILTER_HIDDEN), HY_FILTER_HIDDEN ** -0.5),
        'hy_f2_b': nrm((L, HY_INNER_MLPS, HY_FILTER_HIDDEN), 0.5),
        'hy_f3_w': nrm((L, HY_FILTER_HIDDEN, HY_ORDER * 2 * HY_WIDTH), 0.02 * HY_FILTER_HIDDEN ** -0.5),
        'hy_freq': 1.0 + nrm((L, HY_FILTER_HIDDEN), 0.02),
        'hy_skip': nrm((L, HY_ORDER, HY_WIDTH), 0.5),
        'sw_sink': nrm((L, SW_HEADS), 0.5),
        'mem_w_kv': nrm((L, D, 2 * MQ_COLS), D ** -0.5),
        'w_br_hy': nrm((L, HY_WIDTH, D), HY_WIDTH ** -0.5),
        'w_br_swa': nrm((L, Q_COLS, D), Q_COLS ** -0.5),
        'w_br_mem': nrm((L, MQ_COLS, D), MQ_COLS ** -0.5),
        'w_out': nrm((L, D, D), DEEPNORM_BETA * D ** -0.5),
        'ln1_g': 1.0 + nrm((L, D), 0.02),
        'ln1_b': nrm((L, D), 0.02),
        'router_w': nrm((L, D, N_EXPERTS), D ** -0.5),
        'router_b': nrm((L, N_EXPERTS), 0.01),
        'moe_w1': nrm((L, N_EXPERTS, D, 2 * D_EXPERT), D ** -0.5),
        'moe_b1': nrm((L, N_EXPERTS, 2 * D_EXPERT), 0.01),
        'moe_w2': nrm((L, N_EXPERTS, D_EXPERT, D), DEEPNORM_BETA * D_EXPERT ** -0.5),
        'moe_b2': nrm((L, N_EXPERTS, D), 0.01),
        'ln2_g': 1.0 + nrm((L, D), 0.02),
        'ln2_b': nrm((L, D), 0.02),
    }


def reference(x, mem, ln_in_g, ln_in_b, w_in, b_gate, hy_conv_w, hy_conv_b, hy_f1_w, hy_f1_b,
              hy_f2_w, hy_f2_b, hy_f3_w, hy_freq, hy_skip, sw_sink, mem_w_kv, w_br_hy, w_br_swa,
              w_br_mem, w_out, ln1_g, ln1_b, router_w, router_b, moe_w1, moe_b1, moe_w2, moe_b2,
              ln2_g, ln2_b):
    B, S, D = x.shape
    pos = jnp.arange(S, dtype=jnp.int32)
    x = layer_norm(x, ln_in_g, ln_in_b)
    for l in range(DEPTH):
        h = mixer_block(x, mem, pos, w_in[l], b_gate[l], hy_conv_w[l], hy_conv_b[l], hy_f1_w[l],
                        hy_f1_b[l], hy_f2_w[l], hy_f2_b[l], hy_f3_w[l], hy_freq[l], hy_skip[l],
                        sw_sink[l], mem_w_kv[l], w_br_hy[l], w_br_swa[l], w_br_mem[l], w_out[l])
        x = layer_norm(DEEPNORM_ALPHA * x + h, ln1_g[l], ln1_b[l])
        f = moe_ffn(x.reshape(B * S, D), router_w[l], router_b[l], moe_w1[l], moe_b1[l],
                    moe_w2[l], moe_b2[l]).reshape(B, S, D)
        x = layer_norm(DEEPNORM_ALPHA * x + f, ln2_g[l], ln2_b[l])
    return x
```

```python
import functools
import math

import numpy as np
import jax
import jax.numpy as jnp
from jax import lax
from jax.experimental import pallas as pl
from jax.experimental.pallas import tpu as pltpu

BF = jnp.bfloat16
F32 = jnp.float32

HY_ORDER = 2
HY_SHORT = 3
HY_EMB = 33
HY_BANDS = (HY_EMB - 1) // 2
HY_FAST_DECAY = 0.3
HY_SLOW_DECAY = 1.5
HY_DECAY_TARGET = 1e-2
SW_HEADS = 8
SW_KV_HEADS = 2
SW_HEAD_DIM = 64
SW_WINDOW = 128
ROPE_THETA = 500000.0
ROPE_DIM = SW_HEAD_DIM // 4
MEM_HEADS = 4
MEM_HEAD_DIM = 128
N_BRANCH = 3
TOP_K = 4
SWIGLU_ALPHA = 1.702
SWIGLU_LIMIT = 7.0
MOE_BLOCK = 256
LN_EPS = 1e-5
NEG_INF = -1e30

LANES = 128
V7X_VMEM_BYTES = 64 * 1024 * 1024
VMEM_LIMIT = 48 * 1024 * 1024
DFT_N2 = 128


def _cparams(*sem):
    return pltpu.CompilerParams(dimension_semantics=sem, vmem_limit_bytes=VMEM_LIMIT)


def _ln_rows(r, g, b):
    mu = jnp.mean(r, axis=-1, keepdims=True)
    d = r - mu
    var = jnp.mean(d * d, axis=-1, keepdims=True)
    return d * lax.rsqrt(var + LN_EPS) * g + b


def _ln_kernel(x_ref, g_ref, b_ref, y_ref, yb_ref):
    y = _ln_rows(x_ref[...], g_ref[...], b_ref[...])
    y_ref[...] = y
    yb_ref[...] = y.astype(BF)


def _ln_call(x, g, b, tm=512):
    n, d = x.shape
    row = pl.BlockSpec((tm, d), lambda i: (i, 0))
    vec = pl.BlockSpec((1, d), lambda i: (0, 0))
    return pl.pallas_call(
        _ln_kernel, grid=(n // tm,), in_specs=[row, vec, vec], out_specs=[row, row],
        out_shape=[jax.ShapeDtypeStruct((n, d), F32), jax.ShapeDtypeStruct((n, d), BF)],
        compiler_params=_cparams("parallel"), name="ln_entry",
    )(x, g.reshape(1, d), b.reshape(1, d))


def _combine_ln_kernel(x_ref, y_ref, w_ref, g_ref, b_ref, o_ref, ob_ref, *, alpha):
    w = w_ref[...]
    f = y_ref[0].astype(F32) * w[:, 0:1]
    for k in range(1, TOP_K):
        f = f + y_ref[k].astype(F32) * w[:, k:k + 1]
    y = _ln_rows(alpha * x_ref[...] + f, g_ref[...], b_ref[...])
    o_ref[...] = y
    ob_ref[...] = y.astype(BF)


def _combine_ln_call(x, yg, w, g, b, alpha, tm=512):
    n, d = x.shape
    row = pl.BlockSpec((tm, d), lambda i: (i, 0))
    vec = pl.BlockSpec((1, d), lambda i: (0, 0))
    return pl.pallas_call(
        functools.partial(_combine_ln_kernel, alpha=alpha), grid=(n // tm,),
        in_specs=[row, pl.BlockSpec((TOP_K, tm, d), lambda i: (0, i, 0)),
                  pl.BlockSpec((tm, TOP_K), lambda i: (i, 0)), vec, vec],
        out_specs=[row, row],
        out_shape=[jax.ShapeDtypeStruct((n, d), F32), jax.ShapeDtypeStruct((n, d), BF)],
        compiler_params=_cparams("parallel"), name="moe_combine_ln",
    )(x, yg, w, g.reshape(1, d), b.reshape(1, d))


def _mm_kernel(a_ref, b_ref, o_ref):
    o_ref[...] = jnp.dot(a_ref[...], b_ref[...], preferred_element_type=F32).astype(o_ref.dtype)


def _mm_call(a, b, out_dtype, tm, tn, name):
    m, k = a.shape
    n = b.shape[1]
    return pl.pallas_call(
        _mm_kernel, grid=(n // tn, m // tm),
        in_specs=[pl.BlockSpec((tm, k), lambda j, i: (i, 0)), pl.BlockSpec((k, tn), lambda j, i: (0, j))],
        out_specs=pl.BlockSpec((tm, tn), lambda j, i: (i, j)),
        out_shape=jax.ShapeDtypeStruct((m, n), out_dtype),
        compiler_params=_cparams("parallel", "parallel"), name=name,
    )(a, b)


def _shortconv_kernel(zp_ref, z_ref, zn_ref, w_ref, b_ref, hv_ref, hx1_ref, hx2_ref, *, tl, halo, width):
    i = pl.program_id(1)
    nt = pl.num_programs(1)
    z = z_ref[...].astype(F32)
    prev_row = zp_ref[halo - 1:halo, :].astype(F32)
    next_row = zn_ref[0:1, :].astype(F32)
    prev_row = jnp.where(i == 0, 0.0, prev_row)
    next_row = jnp.where(i == nt - 1, 0.0, next_row)
    row = lax.broadcasted_iota(jnp.int32, z.shape, 0)
    zm1 = jnp.where(row == 0, prev_row, pltpu.roll(z, 1, 0))
    zp1 = jnp.where(row == tl - 1, next_row, pltpu.roll(z, tl - 1, 0))
    w = w_ref[...]
    out = zm1 * w[0:1] + b_ref[...] + z * w[1:2] + zp1 * w[2:3]
    hv_ref[...] = out[:, 0:width].astype(BF)
    hx1_ref[...] = out[:, width:2 * width].astype(BF)
    hx2_ref[...] = out[:, 2 * width:3 * width].astype(BF)


def _shortconv_call(proj, conv_w, conv_b, batch, seq, col_blk, tl=512, halo=16):
    n = proj.shape[0]
    c3 = conv_w.shape[1]
    width = c3 // 3
    nt = seq // tl
    hb = tl // halo
    nhb = seq // halo
    main = pl.BlockSpec((tl, c3), lambda b, i: (b * nt + i, col_blk))
    prev = pl.BlockSpec((halo, c3), lambda b, i: (b * nhb + jnp.maximum(i * hb - 1, 0), col_blk))
    nxt = pl.BlockSpec((halo, c3), lambda b, i: (b * nhb + jnp.minimum((i + 1) * hb, nhb - 1), col_blk))
    out = pl.BlockSpec((tl, width), lambda b, i: (b * nt + i, 0))
    return pl.pallas_call(
        functools.partial(_shortconv_kernel, tl=tl, halo=halo, width=width), grid=(batch, nt),
        in_specs=[prev, main, nxt, pl.BlockSpec((HY_SHORT, c3), lambda b, i: (0, 0)),
                  pl.BlockSpec((1, c3), lambda b, i: (0, 0))],
        out_specs=[out, out, out],
        out_shape=[jax.ShapeDtypeStruct((n, width), BF)] * 3,
        compiler_params=_cparams("parallel", "parallel"), name="hy_shortconv",
    )(proj, proj, proj, conv_w, conv_b.reshape(1, c3))


def _filter_kernel(z_ref, f1w_ref, f1b_ref, f2w_ref, f2b_ref, f3w_ref, freq_ref, adel_ref, k_ref,
                   *, tr, seq, n_inner):
    i = pl.program_id(0)
    hp = lax.Precision.HIGHEST
    z = z_ref[...]
    fr = freq_ref[...]
    h = jnp.sin(fr * (jnp.dot(z, f1w_ref[...], precision=hp, preferred_element_type=F32) + f1b_ref[...]))
    for j in range(n_inner):
        h = jnp.sin(fr * (jnp.dot(h, f2w_ref[j], precision=hp, preferred_element_type=F32) + f2b_ref[j]))
    k = jnp.dot(h, f3w_ref[...], precision=hp, preferred_element_type=F32)
    win = jnp.exp(-z[:, 0:1] * adel_ref[...])
    k = k * jnp.concatenate([win] * HY_ORDER, axis=1)
    row = i * tr + lax.broadcasted_iota(jnp.int32, k.shape, 0)
    k_ref[...] = jnp.where(row == seq, 0.0, k)


def _filter_call(zfeat, f1w, f1b, f2w, f2b, f3w_dir, freq, adel, seq, tr=1024):
    n2l = zfeat.shape[0]
    hid = f1w.shape[1]
    n_inner = f2w.shape[0]
    ow = f3w_dir.shape[2]
    width = adel.shape[1]
    half = (n2l // tr) // 2
    full = lambda *shape: pl.BlockSpec(shape, lambda i: (0,) * len(shape))
    return pl.pallas_call(
        functools.partial(_filter_kernel, tr=tr, seq=seq, n_inner=n_inner), grid=(n2l // tr,),
        in_specs=[pl.BlockSpec((tr, LANES), lambda i: (i, 0)), full(LANES, hid), full(1, hid),
                  full(n_inner, hid, hid), full(n_inner, 1, hid),
                  pl.BlockSpec((None, hid, ow), lambda i: (i // half, 0, 0)),
                  full(1, hid), full(1, width)],
        out_specs=pl.BlockSpec((tr, ow), lambda i: (i, 0)),
        out_shape=jax.ShapeDtypeStruct((n2l, ow), F32),
        compiler_params=_cparams("parallel"), name="hy_filter",
    )(zfeat, f1w, f1b, f2w, f2b, f3w_dir, freq, adel)


def _dft_consts(n1, n2):
    n = n1 * n2
    h1 = n1 // 2
    k1 = np.arange(n1)[:, None]
    ang = -2.0 * np.pi * (k1 * np.arange(h1)[None, :]) / n1
    cr, ci = np.cos(ang), np.sin(ang)
    e1 = np.block([[cr, -ci], [ci, cr]])
    angf = -2.0 * np.pi * (k1 * np.arange(n1)[None, :]) / n1
    ef = np.concatenate([np.cos(angf), np.sin(angf)], axis=0)
    a2 = -2.0 * np.pi * (np.arange(n2)[:, None] * np.arange(n2)[None, :]) / n2
    f2r, f2i = np.cos(a2), np.sin(a2)
    dm = np.block([[f2r, f2i], [-f2i, f2r]])
    at = -2.0 * np.pi * (np.arange(n1)[:, None] * np.arange(n2)[None, :]) / n
    twr, twi = np.cos(at), np.sin(at)
    ab = 2.0 * np.pi * (np.arange(h1)[:, None] * np.arange(n1)[None, :]) / n1
    br, bi = np.cos(ab) / n, np.sin(ab) / n
    f = lambda a: jnp.asarray(a, dtype=F32)
    return dict(e1=f(e1), ef=f(ef), f2r=f(f2r), f2i=f(f2i), dm=f(dm),
                twr=f(twr.reshape(n1, 1, n2)), twi=f(twi.reshape(n1, 1, n2)),
                ctr=f(twr.T.reshape(n2, 1, n1)), cti=f(-twi.T.reshape(n2, 1, n1)),
                br=f(br), bi=f(bi))


def _dft_s1_kernel(e_ref, top_ref, bot_ref, a_ref):
    rhs = jnp.concatenate([top_ref[...], bot_ref[...]], axis=0).astype(BF)
    res = jnp.dot(e_ref[...], rhs, preferred_element_type=F32)
    n1 = a_ref.shape[1]
    a_ref[0] = res[:n1].astype(a_ref.dtype)
    a_ref[1] = res[n1:].astype(a_ref.dtype)


def _dft_s1_call(e_mat, x3, top_idx, bot_idx, n1, tn=2048):
    h1, cols = x3.shape[1], x3.shape[2]
    return pl.pallas_call(
        _dft_s1_kernel, grid=(cols // tn,),
        in_specs=[pl.BlockSpec((2 * n1, 2 * h1), lambda j: (0, 0)),
                  pl.BlockSpec((None, h1, tn), lambda j: (top_idx, 0, j)),
                  pl.BlockSpec((None, h1, tn), lambda j: (bot_idx, 0, j))],
        out_specs=pl.BlockSpec((2, n1, tn), lambda j: (0, 0, j)),
        out_shape=jax.ShapeDtypeStruct((2, n1, cols), BF),
        compiler_params=_cparams("parallel"), name="hy_dft_stage1",
    )(e_mat.astype(BF), x3, x3)


def _twiddled_f2(f2r, f2i, tr, ti):
    gr = f2r * tr - f2i * ti
    gi = f2r * ti + f2i * tr
    return jnp.concatenate([jnp.concatenate([gr, -gi], axis=1), jnp.concatenate([gi, gr], axis=1)], axis=0)


def _dft_s2_filter_kernel(a_ref, f2r_ref, f2i_ref, twr_ref, twi_ref, kf_ref, *, kb):
    n2 = f2r_ref.shape[0]
    for j in range(kb):
        gm = _twiddled_f2(f2r_ref[...], f2i_ref[...], twr_ref[j], twi_ref[j]).astype(BF)
        rhs = jnp.concatenate([a_ref[0, j], a_ref[1, j]], axis=0)
        x = jnp.dot(gm, rhs, preferred_element_type=F32)
        kf_ref[0, j] = x[:n2]
        kf_ref[1, j] = x[n2:]


def _dft_s2_filter_call(a4, cst, kb=8):
    _, n1, n2, c = a4.shape
    blk = pl.BlockSpec((2, kb, n2, c), lambda i: (0, i, 0, 0))
    sq = pl.BlockSpec((n2, n2), lambda i: (0, 0))
    tw = pl.BlockSpec((kb, 1, n2), lambda i: (i, 0, 0))
    return pl.pallas_call(
        functools.partial(_dft_s2_filter_kernel, kb=kb), grid=(n1 // kb,),
        in_specs=[blk, sq, sq, tw, tw], out_specs=blk,
        out_shape=jax.ShapeDtypeStruct(a4.shape, F32),
        compiler_params=_cparams("parallel"), name="hy_dft_stage2_filter",
    )(a4, cst["f2r"], cst["f2i"], cst["twr"], cst["twi"])


def _dft_s2_conv_kernel(a_ref, kf_ref, f2r_ref, f2i_ref, dm_ref, twr_ref, twi_ref, b_ref, *, kb):
    n2 = f2r_ref.shape[0]
    dm = dm_ref[...]
    for j in range(kb):
        gm = _twiddled_f2(f2r_ref[...], f2i_ref[...], twr_ref[j], twi_ref[j]).astype(BF)
        rhs = jnp.concatenate([a_ref[0, j], a_ref[1, j]], axis=0)
        x = jnp.dot(gm, rhs, preferred_element_type=F32)
        xr, xi = x[:n2], x[n2:]
        kr, ki = kf_ref[0, j], kf_ref[1, j]
        y = jnp.concatenate([xr * kr - xi * ki, xr * ki + xi * kr], axis=0).astype(BF)
        bm = jnp.dot(dm, y, preferred_element_type=F32)
        b_ref[0, j] = bm[:n2].astype(b_ref.dtype)
        b_ref[1, j] = bm[n2:].astype(b_ref.dtype)


def _dft_s2_conv_call(a4, kf4, order_idx, cst, kb=8):
    _, n1, n2, c = a4.shape
    blk = pl.BlockSpec((2, kb, n2, c), lambda i: (0, i, 0, 0))
    kblk = pl.BlockSpec((None, 2, kb, n2, c), lambda i: (order_idx, 0, i, 0, 0))
    sq = pl.BlockSpec((n2, n2), lambda i: (0, 0))
    tw = pl.BlockSpec((kb, 1, n2), lambda i: (i, 0, 0))
    return pl.pallas_call(
        functools.partial(_dft_s2_conv_kernel, kb=kb), grid=(n1 // kb,),
        in_specs=[blk, kblk, sq, sq, pl.BlockSpec((2 * n2, 2 * n2), lambda i: (0, 0)), tw, tw],
        out_specs=blk, out_shape=jax.ShapeDtypeStruct(a4.shape, BF),
        compiler_params=_cparams("parallel"), name="hy_dft_stage2_conv",
    )(a4, kf4, cst["f2r"], cst["f2i"], cst["dm"].astype(BF), cst["twr"], cst["twi"])


def _dft_s1inv_kernel(b_ref, br_ref, bi_ref, ctr_ref, cti_ref, u_ref, gx_ref, skip_ref, o_ref, *, ng, c):
    h1 = br_ref.shape[0]
    skip = skip_ref[...]
    for j in range(ng):
        tr, ti = ctr_ref[j], cti_ref[j]
        mr = br_ref[...] * tr - bi_ref[...] * ti
        mi = br_ref[...] * ti + bi_ref[...] * tr
        em = jnp.concatenate([jnp.concatenate([mr, -mi], axis=1),
                              jnp.concatenate([mi, mr], axis=1)], axis=0).astype(BF)
        y = jnp.dot(em, b_ref[:, j * c:(j + 1) * c], preferred_element_type=F32)
        for b in range(2):
            u = u_ref[b, :, j * c:(j + 1) * c].astype(F32)
            g = gx_ref[b, :, j * c:(j + 1) * c].astype(F32)
            o_ref[b, :, j * c:(j + 1) * c] = (g * (y[b * h1:(b + 1) * h1] + skip * u)).astype(o_ref.dtype)


def _dft_s1inv_call(b2, u3, gx3, skip, cst, c, ng=4):
    rows, cols = b2.shape
    h1 = u3.shape[1]
    n2 = cols // c
    n1 = rows // 2
    ub = pl.BlockSpec((2, h1, ng * c), lambda i: (0, 0, i))
    cb = pl.BlockSpec((h1, n1), lambda i: (0, 0))
    tw = pl.BlockSpec((ng, 1, n1), lambda i: (i, 0, 0))
    return pl.pallas_call(
        functools.partial(_dft_s1inv_kernel, ng=ng, c=c), grid=(n2 // ng,),
        in_specs=[pl.BlockSpec((rows, ng * c), lambda i: (0, i)), cb, cb, tw, tw, ub, ub,
                  pl.BlockSpec((1, c), lambda i: (0, 0))],
        out_specs=ub, out_shape=jax.ShapeDtypeStruct(u3.shape, BF),
        compiler_params=_cparams("parallel"), name="hy_dft_stage1_inverse",
    )(b2, cst["br"], cst["bi"], cst["ctr"], cst["cti"], u3, gx3, skip.reshape(1, c))


def _filter_features(seq):
    t01 = jnp.linspace(0.0, 1.0, seq, dtype=F32)[:, None]
    w = (2.0 * math.pi) * jnp.arange(seq, dtype=F32)[:, None] / seq
    bands = jnp.linspace(1e-4, HY_BANDS - 1, HY_BANDS, dtype=F32)
    z = jnp.concatenate([t01, jnp.cos(bands * w), -jnp.sin(bands * w)], axis=-1)
    z2 = jnp.concatenate([z, z[:1], z[1:][::-1]], axis=0)
    return jnp.pad(z2, ((0, 0), (0, LANES - HY_EMB)))


def _hyena_filter_spectrum(seq, width, f1_w, f1_b, f2_w, f2_b, f3_w, freq, zfeat, cst):
    hid = f1_w.shape[1]
    n1 = 2 * seq // DFT_N2
    max_decay = math.log(HY_DECAY_TARGET) / HY_FAST_DECAY
    min_decay = math.log(HY_DECAY_TARGET) / HY_SLOW_DECAY
    adel = jnp.abs(jnp.linspace(min_decay, max_decay, width, dtype=F32)).reshape(1, width)
    f1w = jnp.pad(f1_w, ((0, LANES - HY_EMB), (0, 0)))
    f3d = f3_w.reshape(hid, HY_ORDER, 2, width).transpose(2, 0, 1, 3).reshape(2, hid, HY_ORDER * width)
    k2 = _filter_call(zfeat, f1w, f1_b.reshape(1, hid), f2_w, f2_b.reshape(-1, 1, hid), f3d,
                      freq.reshape(1, hid), adel, seq)
    kf = []
    for o in range(HY_ORDER):
        ko = k2[:, o * width:(o + 1) * width].reshape(2, n1 // 2, DFT_N2 * width)
        a = _dft_s1_call(cst["ef"], ko, 0, 1, n1)
        kf.append(_dft_s2_filter_call(a.reshape(2, n1, DFT_N2, width), cst))
    return jnp.stack(kf)


def _hyena_conv(u, gx, kf, order_idx, skip, cst, seq, width):
    n1 = 2 * seq // DFT_N2
    h1 = n1 // 2
    u3 = u.reshape(2, h1, DFT_N2 * width)
    a = _dft_s1_call(cst["e1"], u3, 0, 1, n1)
    bm = _dft_s2_conv_call(a.reshape(2, n1, DFT_N2, width), kf, order_idx, cst)
    y = _dft_s1inv_call(bm.reshape(2 * n1, DFT_N2 * width), u3, gx.reshape(2, h1, DFT_N2 * width),
                        skip, cst, width)
    return y.reshape(2 * seq, width)


def _rope_tables(seq):
    half = ROPE_DIM // 2
    inv = jnp.power(jnp.float32(ROPE_THETA), -jnp.arange(half, dtype=F32) * (2.0 / ROPE_DIM))
    ang = jnp.arange(seq, dtype=jnp.int32).astype(F32)[:, None] * inv
    cos, sin = jnp.cos(ang), jnp.sin(ang)
    rest = SW_HEAD_DIM - ROPE_DIM
    ones = jnp.ones((seq, rest), F32)
    zeros = jnp.zeros((seq, rest), F32)
    ch = jnp.concatenate([cos, cos, ones], axis=1)
    sh = jnp.concatenate([-sin, sin, zeros], axis=1)
    reps = LANES // SW_HEAD_DIM
    return jnp.tile(ch, (1, reps)), jnp.tile(sh, (1, reps))


def _rope_swap_matrix():
    half = ROPE_DIM // 2
    p = np.zeros((LANES, LANES), np.float32)
    for j in range(LANES):
        d = j % SW_HEAD_DIM
        if d < half:
            p[j + half, j] = 1.0
        elif d < ROPE_DIM:
            p[j - half, j] = 1.0
    return jnp.asarray(p)


def _rope_kernel(q_ref, k_ref, c_ref, s_ref, p_ref, qo_ref, ko_ref, *, scale):
    c, s, p = c_ref[...], s_ref[...], p_ref[...]

    def rot(x):
        sw = jnp.dot(x, p, preferred_element_type=F32)
        return x.astype(F32) * c + sw * s

    for g in range(q_ref.shape[1] // LANES):
        qo_ref[:, g * LANES:(g + 1) * LANES] = (rot(q_ref[:, g * LANES:(g + 1) * LANES]) * scale).astype(BF)
    for g in range(k_ref.shape[1] // LANES):
        ko_ref[:, g * LANES:(g + 1) * LANES] = rot(k_ref[:, g * LANES:(g + 1) * LANES]).astype(BF)


def _rope_call(proj, cos_t, sin_t, swap, batch, seq, q_blk, k_blk, qw, kw, tm=512):
    n = proj.shape[0]
    nt = seq // tm
    tab = pl.BlockSpec((tm, LANES), lambda b, i: (i, 0))
    return pl.pallas_call(
        functools.partial(_rope_kernel, scale=SW_HEAD_DIM ** -0.5), grid=(batch, nt),
        in_specs=[pl.BlockSpec((tm, qw), lambda b, i: (b * nt + i, q_blk)),
                  pl.BlockSpec((tm, kw), lambda b, i: (b * nt + i, k_blk)),
                  tab, tab, pl.BlockSpec((LANES, LANES), lambda b, i: (0, 0))],
        out_specs=[pl.BlockSpec((tm, qw), lambda b, i: (b * nt + i, 0)),
                   pl.BlockSpec((tm, kw), lambda b, i: (b * nt + i, 0))],
        out_shape=[jax.ShapeDtypeStruct((n, qw), BF), jax.ShapeDtypeStruct((n, kw), BF)],
        compiler_params=_cparams("parallel", "parallel"), name="sw_rope",
    )(proj, proj, cos_t, sin_t, swap.astype(BF))


def _attn_kernel(sink_ref, q_ref, kp_ref, km_ref, kn_ref, vp_ref, vm_ref, vn_ref, o_ref, *, tq, seq):
    blk = SW_WINDOW
    i = pl.program_id(1)
    kwin = jnp.concatenate([kp_ref[...], km_ref[...], kn_ref[...]], axis=0)
    vwin = jnp.concatenate([vp_ref[...], vm_ref[...], vn_ref[...]], axis=0)
    lo = lax.broadcasted_iota(jnp.int32, (tq + 2 * blk, LANES), 1) < SW_HEAD_DIM
    zero = jnp.zeros((), BF)
    kmask, vmask = [], []
    for h in range(SW_KV_HEADS):
        kh = kwin[:, h * LANES:(h + 1) * LANES]
        vh = vwin[:, h * LANES:(h + 1) * LANES]
        kmask.append((jnp.where(lo, kh, zero), jnp.where(lo, zero, kh)))
        vmask.append((jnp.where(lo, vh, zero), jnp.where(lo, zero, vh)))
    a_idx = lax.broadcasted_iota(jnp.int32, (blk, 3 * blk), 0)
    w_idx = lax.broadcasted_iota(jnp.int32, (blk, 3 * blk), 1)
    rel = w_idx - a_idx
    groups_per_kv = (SW_HEADS // SW_KV_HEADS) * SW_HEAD_DIM // LANES
    for jb in range(tq // blk):
        kpos = i * tq + (jb - 1) * blk + w_idx
        bad = (rel < 0) | (rel > 2 * SW_WINDOW) | (kpos < 0) | (kpos >= seq)
        for g in range(q_ref.shape[1] // LANES):
            h = g // groups_per_kv
            qg = q_ref[jb * blk:(jb + 1) * blk, g * LANES:(g + 1) * LANES]
            acc = jnp.zeros((blk, LANES), F32)
            for par in range(LANES // SW_HEAD_DIM):
                kx = kmask[h][par][jb * blk:(jb + 3) * blk]
                vx = vmask[h][par][jb * blk:(jb + 3) * blk]
                s = lax.dot_general(qg, kx, (((1,), (1,)), ((), ())), preferred_element_type=F32)
                s = jnp.where(bad, NEG_INF, s)
                sk = sink_ref[g * (LANES // SW_HEAD_DIM) + par]
                m = jnp.maximum(jnp.max(s, axis=-1, keepdims=True), sk)
                p = jnp.exp(s - m)
                denom = jnp.sum(p, axis=-1, keepdims=True) + jnp.exp(sk - m)
                acc = acc + jnp.dot(p.astype(BF), vx, preferred_element_type=F32) / denom
            o_ref[jb * blk:(jb + 1) * blk, g * LANES:(g + 1) * LANES] = acc.astype(BF)


def _attn_call(q_rot, k_rot, proj, v_blk, sink, batch, seq, tq=256):
    n, qw = q_rot.shape
    kw = k_rot.shape[1]
    blk = SW_WINDOW
    nt = seq // tq
    r = tq // blk
    nb = seq // blk
    main = lambda cb: pl.BlockSpec((tq, kw), lambda b, i, s: (b * nt + i, cb))
    prev = lambda cb: pl.BlockSpec((blk, kw), lambda b, i, s: (b * nb + jnp.maximum(i * r - 1, 0), cb))
    nxt = lambda cb: pl.BlockSpec((blk, kw), lambda b, i, s: (b * nb + jnp.minimum((i + 1) * r, nb - 1), cb))
    qspec = pl.BlockSpec((tq, qw), lambda b, i, s: (b * nt + i, 0))
    grid_spec = pltpu.PrefetchScalarGridSpec(
        num_scalar_prefetch=1, grid=(batch, nt),
        in_specs=[qspec, prev(0), main(0), nxt(0), prev(v_blk), main(v_blk), nxt(v_blk)],
        out_specs=qspec)
    return pl.pallas_call(
        functools.partial(_attn_kernel, tq=tq, seq=seq), grid_spec=grid_spec,
        out_shape=jax.ShapeDtypeStruct((n, qw), BF),
        compiler_params=_cparams("parallel", "parallel"), name="sw_attention",
    )(sink, q_rot, k_rot, k_rot, k_rot, proj, proj, proj)


def _memattn_kernel(q_ref, mk_ref, mv_ref, o_ref):
    scale = MEM_HEAD_DIM ** -0.5
    for h in range(MEM_HEADS):
        sl = slice(h * MEM_HEAD_DIM, (h + 1) * MEM_HEAD_DIM)
        s = lax.dot_general(q_ref[:, sl], mk_ref[:, sl], (((1,), (1,)), ((), ())),
                            preferred_element_type=F32) * scale
        m = jnp.max(s, axis=-1, keepdims=True)
        p = jnp.exp(s - m)
        denom = jnp.sum(p, axis=-1, keepdims=True)
        o = jnp.dot(p.astype(BF), mv_ref[:, sl], preferred_element_type=F32) / denom
        o_ref[:, sl] = o.astype(BF)


def _memattn_call(proj, q_blk, memkv, batch, seq, mem_tokens, tm=512):
    n = proj.shape[0]
    w = MEM_HEADS * MEM_HEAD_DIM
    nt = seq // tm
    return pl.pallas_call(
        _memattn_kernel, grid=(batch, nt),
        in_specs=[pl.BlockSpec((tm, w), lambda b, i: (b * nt + i, q_blk)),
                  pl.BlockSpec((mem_tokens, w), lambda b, i: (b, 0)),
                  pl.BlockSpec((mem_tokens, w), lambda b, i: (b, 1))],
        out_specs=pl.BlockSpec((tm, w), lambda b, i: (b * nt + i, 0)),
        out_shape=jax.ShapeDtypeStruct((n, w), BF),
        compiler_params=_cparams("parallel", "parallel"), name="mem_attention",
    )(proj, memkv, memkv)


def _merge_kernel(x_ref, gl_ref, bg_ref, yh_ref, ys_ref, ym_ref, wh_ref, ws_ref, wm_ref, wo_ref,
                  g_ref, b_ref, o_ref, ob_ref, *, alpha, d):
    merged = None
    for br, (y_ref, w_ref) in enumerate(((yh_ref, wh_ref), (ys_ref, ws_ref), (ym_ref, wm_ref))):
        gate = jax.nn.sigmoid(gl_ref[:, br * d:(br + 1) * d].astype(F32) + bg_ref[br:br + 1, :])
        t = gate * jnp.dot(y_ref[...], w_ref[...], preferred_element_type=F32)
        merged = t if merged is None else merged + t
    h = jnp.dot(merged.astype(BF), wo_ref[...], preferred_element_type=F32)
    y = _ln_rows(alpha * x_ref[...] + h, g_ref[...], b_ref[...])
    o_ref[...] = y
    ob_ref[...] = y.astype(BF)


def _merge_call(x, proj, gate_blk, b_gate, y_hy, y_sw, y_mem, w_hy, w_sw, w_mem, w_out, g, b, alpha, tm=256):
    n, d = x.shape
    row = lambda w: pl.BlockSpec((tm, w), lambda i: (i, 0))
    full = lambda a: pl.BlockSpec(a.shape, lambda i: (0,) * a.ndim)
    g2, b2 = g.reshape(1, d), b.reshape(1, d)
    return pl.pallas_call(
        functools.partial(_merge_kernel, alpha=alpha, d=d), grid=(n // tm,),
        in_specs=[row(d), pl.BlockSpec((tm, N_BRANCH * d), lambda i: (i, gate_blk)), full(b_gate),
                  row(y_hy.shape[1]), row(y_sw.shape[1]), row(y_mem.shape[1]),
                  full(w_hy), full(w_sw), full(w_mem), full(w_out), full(g2), full(b2)],
        out_specs=[row(d), row(d)],
        out_shape=[jax.ShapeDtypeStruct((n, d), F32), jax.ShapeDtypeStruct((n, d), BF)],
        compiler_params=_cparams("parallel"), name="merge_outproj_ln",
    )(x, proj, b_gate, y_hy, y_sw, y_mem, w_hy, w_sw, w_mem, w_out, g2, b2)


def _expert_kernel(be_ref, x_ref, w1g_ref, w1l_ref, b1g_ref, b1l_ref, w2_ref, b2_ref, y_ref):
    x = x_ref[...]
    hg = jnp.dot(x, w1g_ref[...], preferred_element_type=F32) + b1g_ref[...]
    hl = jnp.dot(x, w1l_ref[...], preferred_element_type=F32) + b1l_ref[...]
    hg = jnp.minimum(hg, SWIGLU_LIMIT)
    hl = jnp.clip(hl, -SWIGLU_LIMIT, SWIGLU_LIMIT)
    act = hg * jax.nn.sigmoid(SWIGLU_ALPHA * hg) * (hl + 1.0)
    y = jnp.dot(act.astype(BF), w2_ref[...], preferred_element_type=F32) + b2_ref[...]
    y_ref[...] = y.astype(y_ref.dtype)


def _expert_call(block_expert, xs, w1g, w1l, b1g, b1l, w2, b2, layer):
    p, d = xs.shape
    de = w1g.shape[3]
    wspec = lambda r, c: pl.BlockSpec((None, None, r, c), lambda i, be: (layer, be[i], 0, 0))
    grid_spec = pltpu.PrefetchScalarGridSpec(
        num_scalar_prefetch=1, grid=(p // MOE_BLOCK,),
        in_specs=[pl.BlockSpec((MOE_BLOCK, d), lambda i, be: (i, 0)),
                  wspec(d, de), wspec(d, de), wspec(1, de), wspec(1, de), wspec(de, d), wspec(1, d)],
        out_specs=pl.BlockSpec((MOE_BLOCK, d), lambda i, be: (i, 0)))
    return pl.pallas_call(
        _expert_kernel, grid_spec=grid_spec, out_shape=jax.ShapeDtypeStruct((p, d), BF),
        compiler_params=_cparams("arbitrary"), name="moe_experts",
    )(block_expert, xs, w1g, w1l, b1g, b1l, w2, b2)


def _route(x, router_w, router_b, n_experts):
    n = x.shape[0]
    logits = jnp.dot(x, router_w, precision=lax.Precision.HIGHEST) + router_b
    top_v, top_i = lax.top_k(logits, TOP_K)
    gate = jax.nn.softmax(top_v, axis=-1)
    a = n * TOP_K
    n_blocks = -(-(a + n_experts * (MOE_BLOCK - 1)) // MOE_BLOCK)
    e_flat = top_i.reshape(-1).astype(jnp.int32)
    onehot = (e_flat[:, None] == jnp.arange(n_experts, dtype=jnp.int32)[None, :]).astype(jnp.int32)
    csum = jnp.cumsum(onehot, axis=0)
    rank = jnp.sum(csum * onehot, axis=1) - 1
    counts = csum[-1]
    padded = (counts + MOE_BLOCK - 1) // MOE_BLOCK * MOE_BLOCK
    pad_end = jnp.cumsum(padded)
    dest = (pad_end - padded)[e_flat] + rank
    tok = jnp.arange(a, dtype=jnp.int32) // TOP_K
    slot_tok = jnp.zeros((n_blocks * MOE_BLOCK,), jnp.int32).at[dest].set(tok)
    block_expert = jnp.minimum(
        jnp.searchsorted(pad_end, jnp.arange(n_blocks, dtype=pad_end.dtype) * MOE_BLOCK, side='right'),
        n_experts - 1).astype(jnp.int32)
    return gate, dest.reshape(n, TOP_K), slot_tok, block_expert


def kernel(x, mem, ln_in_g, ln_in_b, w_in, b_gate, hy_conv_w, hy_conv_b, hy_f1_w, hy_f1_b, hy_f2_w, hy_f2_b, hy_f3_w, hy_freq, hy_skip, sw_sink, mem_w_kv, w_br_hy, w_br_swa, w_br_mem, w_out, ln1_g, ln1_b, router_w, router_b, moe_w1, moe_b1, moe_w2, moe_b2, ln2_g, ln2_b):
    batch, seq, d = x.shape
    depth = w_in.shape[0]
    n = batch * seq
    width = hy_skip.shape[2]
    hy_cols = 3 * width
    q_cols = SW_HEADS * SW_HEAD_DIM
    kv_cols = SW_KV_HEADS * SW_HEAD_DIM
    mq_cols = MEM_HEADS * MEM_HEAD_DIM
    gate_cols = N_BRANCH * d
    mem_tokens = mem.shape[1]
    n_experts = router_w.shape[2]
    alpha = (2 * depth) ** 0.25
    assert batch == 2, "the long convolution packs exactly two batch elements into one complex signal"
    assert w_in.shape[2] == hy_cols + q_cols + 2 * kv_cols + mq_cols + gate_cols
    n1 = 2 * seq // DFT_N2

    cst = _dft_consts(n1, DFT_N2)
    zfeat = _filter_features(seq)
    cos_t, sin_t = _rope_tables(seq)
    swap = _rope_swap_matrix()

    o_q = hy_cols
    o_k = o_q + q_cols
    o_v = o_k + kv_cols
    o_mq = o_v + kv_cols
    o_g = o_mq + mq_cols
    dup = lambda w0: jnp.concatenate(
        [w_in[:, :, w0 + h * SW_HEAD_DIM:w0 + (h + 1) * SW_HEAD_DIM] for h in range(SW_KV_HEADS) for _ in range(2)],
        axis=2)
    w_proj = jnp.concatenate([w_in[:, :, o_g:o_g + gate_cols], w_in[:, :, :hy_cols], w_in[:, :, o_q:o_q + q_cols],
                              w_in[:, :, o_mq:o_mq + mq_cols], dup(o_k), dup(o_v)], axis=2).astype(BF)
    kw = 2 * kv_cols
    c_hy, c_q, c_mq = gate_cols, gate_cols + hy_cols, gate_cols + hy_cols + q_cols
    c_k = c_mq + mq_cols
    c_v = c_k + kw
    proj_cols = c_v + kw
    tn_proj = proj_cols // 4

    w1g = moe_w1[..., 0::2].astype(BF)
    w1l = moe_w1[..., 1::2].astype(BF)
    de = w1g.shape[-1]
    b1g = moe_b1[..., 0::2].reshape(depth, n_experts, 1, de)
    b1l = moe_b1[..., 1::2].reshape(depth, n_experts, 1, de)
    w2 = moe_w2.astype(BF)
    b2 = moe_b2.reshape(depth, n_experts, 1, d)
    mem_bf = mem.reshape(batch * mem_tokens, d).astype(BF)

    xf, xb = _ln_call(x.reshape(n, d), ln_in_g, ln_in_b)
    for l in range(depth):
        proj = _mm_call(xb, w_proj[l], BF, 1024, tn_proj, "in_proj")

        hv, hx1, hx2 = _shortconv_call(proj, hy_conv_w[l], hy_conv_b[l], batch, seq, c_hy // hy_cols)
        kf = _hyena_filter_spectrum(seq, width, hy_f1_w[l], hy_f1_b[l], hy_f2_w[l], hy_f2_b[l],
                                    hy_f3_w[l], hy_freq[l], zfeat, cst)
        y_hy = hv
        for o, gx in enumerate((hx1, hx2)):
            y_hy = _hyena_conv(y_hy, gx, kf, o, hy_skip[l, o], cst, seq, width)

        q_rot, k_rot = _rope_call(proj, cos_t, sin_t, swap, batch, seq, c_q // q_cols, c_k // kw, q_cols, kw)
        y_sw = _attn_call(q_rot, k_rot, proj, c_v // kw, sw_sink[l], batch, seq)

        memkv = _mm_call(mem_bf, mem_w_kv[l].astype(BF), BF, mem_tokens, 2 * mq_cols, "mem_kv")
        y_mem = _memattn_call(proj, c_mq // mq_cols, memkv, batch, seq, mem_tokens)

        xf, xb = _merge_call(xf, proj, 0, b_gate[l], y_hy, y_sw, y_mem, w_br_hy[l].astype(BF),
                             w_br_swa[l].astype(BF), w_br_mem[l].astype(BF), w_out[l].astype(BF),
                             ln1_g[l], ln1_b[l], alpha)

        gate, dest, slot_tok, block_expert = _route(xf, router_w[l], router_b[l], n_experts)
        xs = jnp.take(xb, slot_tok, axis=0)
        ys = _expert_call(block_expert, xs, w1g, w1l, b1g, b1l, w2, b2, l)
        yg = jnp.take(ys, dest.T, axis=0)
        xf, xb = _combine_ln_call(xf, yg, gate, ln2_g[l], ln2_b[l], alpha)
    return xf.reshape(batch, seq, d)
```

```python
import functools
import math

import numpy as np
import jax
import jax.numpy as jnp
from jax import lax
from jax.experimental import pallas as pl
from jax.experimental.pallas import tpu as pltpu

BF = jnp.bfloat16
F32 = jnp.float32

HY_ORDER = 2
HY_SHORT = 3
HY_EMB = 33
HY_BANDS = (HY_EMB - 1) // 2
HY_FAST_DECAY = 0.3
HY_SLOW_DECAY = 1.5
HY_DECAY_TARGET = 1e-2
SW_HEADS = 8
SW_KV_HEADS = 2
SW_HEAD_DIM = 64
SW_WINDOW = 128
ROPE_THETA = 500000.0
ROPE_DIM = SW_HEAD_DIM // 4
MEM_HEADS = 4
MEM_HEAD_DIM = 128
N_BRANCH = 3
TOP_K = 4
SWIGLU_ALPHA = 1.702
SWIGLU_LIMIT = 7.0
MOE_BLOCK = 256
LN_EPS = 1e-5
NEG_INF = -1e30

LANES = 128
V7X_VMEM_BYTES = 64 * 1024 * 1024
VMEM_LIMIT = 48 * 1024 * 1024
DFT_N2 = 128


def _cparams(*sem):
    return pltpu.CompilerParams(dimension_semantics=sem, vmem_limit_bytes=VMEM_LIMIT)


def _ln_rows(r, g, b):
    mu = jnp.mean(r, axis=-1, keepdims=True)
    d = r - mu
    var = jnp.mean(d * d, axis=-1, keepdims=True)
    return d * lax.rsqrt(var + LN_EPS) * g + b


def _ln_kernel(x_ref, g_ref, b_ref, y_ref, yb_ref):
    y = _ln_rows(x_ref[...], g_ref[...], b_ref[...])
    y_ref[...] = y
    yb_ref[...] = y.astype(BF)


def _ln_call(x, g, b, tm=512):
    n, d = x.shape
    row = pl.BlockSpec((tm, d), lambda i: (i, 0))
    vec = pl.BlockSpec((1, d), lambda i: (0, 0))
    return pl.pallas_call(
        _ln_kernel, grid=(n // tm,), in_specs=[row, vec, vec], out_specs=[row, row],
        out_shape=[jax.ShapeDtypeStruct((n, d), F32), jax.ShapeDtypeStruct((n, d), BF)],
        compiler_params=_cparams("parallel"), name="ln_entry",
    )(x, g.reshape(1, d), b.reshape(1, d))


def _combine_ln_kernel(x_ref, y_ref, w_ref, g_ref, b_ref, o_ref, ob_ref, *, alpha):
    w = w_ref[...]
    f = y_ref[0].astype(F32) * w[:, 0:1]
    for k in range(1, TOP_K):
        f = f + y_ref[k].astype(F32) * w[:, k:k + 1]
    y = _ln_rows(alpha * x_ref[...] + f, g_ref[...], b_ref[...])
    o_ref[...] = y
    ob_ref[...] = y.astype(BF)


def _combine_ln_call(x, yg, w, g, b, alpha, tm=512):
    n, d = x.shape
    row = pl.BlockSpec((tm, d), lambda i: (i, 0))
    vec = pl.BlockSpec((1, d), lambda i: (0, 0))
    return pl.pallas_call(
        functools.partial(_combine_ln_kernel, alpha=alpha), grid=(n // tm,),
        in_specs=[row, pl.BlockSpec((TOP_K, tm, d), lambda i: (0, i, 0)),
                  pl.BlockSpec((tm, TOP_K), lambda i: (i, 0)), vec, vec],
        out_specs=[row, row],
        out_shape=[jax.ShapeDtypeStruct((n, d), F32), jax.ShapeDtypeStruct((n, d), BF)],
        compiler_params=_cparams("parallel"), name="moe_combine_ln",
    )(x, yg, w, g.reshape(1, d), b.reshape(1, d))


def _mm_kernel(a_ref, b_ref, o_ref):
    o_ref[...] = jnp.dot(a_ref[...], b_ref[...], preferred_element_type=F32).astype(o_ref.dtype)


def _mm_call(a, b, out_dtype, tm, tn, name):
    m, k = a.shape
    n = b.shape[1]
    return pl.pallas_call(
        _mm_kernel, grid=(n // tn, m // tm),
        in_specs=[pl.BlockSpec((tm, k), lambda j, i: (i, 0)), pl.BlockSpec((k, tn), lambda j, i: (0, j))],
        out_specs=pl.BlockSpec((tm, tn), lambda j, i: (i, j)),
        out_shape=jax.ShapeDtypeStruct((m, n), out_dtype),
        compiler_params=_cparams("parallel", "parallel"), name=name,
    )(a, b)


def _shortconv_kernel(zp_ref, z_ref, zn_ref, w_ref, b_ref, hv_ref, hx1_ref, hx2_ref, *, tl, halo, width):
    i = pl.program_id(1)
    nt = pl.num_programs(1)
    z = z_ref[...].astype(F32)
    prev_row = zp_ref[halo - 1:halo, :].astype(F32)
    next_row = zn_ref[0:1, :].astype(F32)
    prev_row = jnp.where(i == 0, 0.0, prev_row)
    next_row = jnp.where(i == nt - 1, 0.0, next_row)
    row = lax.broadcasted_iota(jnp.int32, z.shape, 0)
    zm1 = jnp.where(row == 0, prev_row, pltpu.roll(z, 1, 0))
    zp1 = jnp.where(row == tl - 1, next_row, pltpu.roll(z, tl - 1, 0))
    w = w_ref[...]
    out = zm1 * w[0:1] + b_ref[...] + z * w[1:2] + zp1 * w[2:3]
    hv_ref[...] = out[:, 0:width].astype(BF)
    hx1_ref[...] = out[:, width:2 * width].astype(BF)
    hx2_ref[...] = out[:, 2 * width:3 * width].astype(BF)


def _shortconv_call(proj, conv_w, conv_b, batch, seq, col_blk, tl=512, halo=16):
    n = proj.shape[0]
    c3 = conv_w.shape[1]
    width = c3 // 3
    nt = seq // tl
    hb = tl // halo
    nhb = seq // halo
    main = pl.BlockSpec((tl, c3), lambda b, i: (b * nt + i, col_blk))
    prev = pl.BlockSpec((halo, c3), lambda b, i: (b * nhb + jnp.maximum(i * hb - 1, 0), col_blk))
    nxt = pl.BlockSpec((halo, c3), lambda b, i: (b * nhb + jnp.minimum((i + 1) * hb, nhb - 1), col_blk))
    out = pl.BlockSpec((tl, width), lambda b, i: (b * nt + i, 0))
    return pl.pallas_call(
        functools.partial(_shortconv_kernel, tl=tl, halo=halo, width=width), grid=(batch, nt),
        in_specs=[prev, main, nxt, pl.BlockSpec((HY_SHORT, c3), lambda b, i: (0, 0)),
                  pl.BlockSpec((1, c3), lambda b, i: (0, 0))],
        out_specs=[out, out, out],
        out_shape=[jax.ShapeDtypeStruct((n, width), BF)] * 3,
        compiler_params=_cparams("parallel", "parallel"), name="hy_shortconv",
    )(proj, proj, proj, conv_w, conv_b.reshape(1, c3))


def _filter_kernel(z_ref, f1w_ref, f1b_ref, f2w_ref, f2b_ref, f3w_ref, freq_ref, adel_ref, k_ref,
                   *, tr, seq, n_inner):
    i = pl.program_id(0)
    hp = lax.Precision.HIGHEST
    z = z_ref[...]
    fr = freq_ref[...]
    h = jnp.sin(fr * (jnp.dot(z, f1w_ref[...], precision=hp, preferred_element_type=F32) + f1b_ref[...]))
    for j in range(n_inner):
        h = jnp.sin(fr * (jnp.dot(h, f2w_ref[j], precision=hp, preferred_element_type=F32) + f2b_ref[j]))
    k = jnp.dot(h, f3w_ref[...], precision=hp, preferred_element_type=F32)
    win = jnp.exp(-z[:, 0:1] * adel_ref[...])
    k = k * jnp.concatenate([win] * HY_ORDER, axis=1)
    row = i * tr + lax.broadcasted_iota(jnp.int32, k.shape, 0)
    k_ref[...] = jnp.where(row == seq, 0.0, k)


def _filter_call(zfeat, f1w, f1b, f2w, f2b, f3w_dir, freq, adel, seq, tr=1024):
    n2l = zfeat.shape[0]
    hid = f1w.shape[1]
    n_inner = f2w.shape[0]
    ow = f3w_dir.shape[2]
    width = adel.shape[1]
    half = (n2l // tr) // 2
    full = lambda *shape: pl.BlockSpec(shape, lambda i: (0,) * len(shape))
    return pl.pallas_call(
        functools.partial(_filter_kernel, tr=tr, seq=seq, n_inner=n_inner), grid=(n2l // tr,),
        in_specs=[pl.BlockSpec((tr, LANES), lambda i: (i, 0)), full(LANES, hid), full(1, hid),
                  full(n_inner, hid, hid), full(n_inner, 1, hid),
                  pl.BlockSpec((None, hid, ow), lambda i: (i // half, 0, 0)),
                  full(1, hid), full(1, width)],
        out_specs=pl.BlockSpec((tr, ow), lambda i: (i, 0)),
        out_shape=jax.ShapeDtypeStruct((n2l, ow), F32),
        compiler_params=_cparams("parallel"), name="hy_filter",
    )(zfeat, f1w, f1b, f2w, f2b, f3w_dir, freq, adel)


def _dft_consts(n1, n2):
    n = n1 * n2
    h1 = n1 // 2
    k1 = np.arange(n1)[:, None]
    ang = -2.0 * np.pi * (k1 * np.arange(h1)[None, :]) / n1
    cr, ci = np.cos(ang), np.sin(ang)
    e1 = np.block([[cr, -ci], [ci, cr]])
    angf = -2.0 * np.pi * (k1 * np.arange(n1)[None, :]) / n1
    ef = np.concatenate([np.cos(angf), np.sin(angf)], axis=0)
    a2 = -2.0 * np.pi * (np.arange(n2)[:, None] * np.arange(n2)[None, :]) / n2
    f2r, f2i = np.cos(a2), np.sin(a2)
    dm = np.block([[f2r, f2i], [-f2i, f2r]])
    at = -2.0 * np.pi * (np.arange(n1)[:, None] * np.arange(n2)[None, :]) / n
    twr, twi = np.cos(at), np.sin(at)
    ab = 2.0 * np.pi * (np.arange(h1)[:, None] * np.arange(n1)[None, :]) / n1
    br, bi = np.cos(ab) / n, np.sin(ab) / n
    f = lambda a: jnp.asarray(a, dtype=F32)
    return dict(e1=f(e1), ef=f(ef), f2r=f(f2r), f2i=f(f2i), dm=f(dm),
                twr=f(twr.reshape(n1, 1, n2)), twi=f(twi.reshape(n1, 1, n2)),
                ctr=f(twr.T.reshape(n2, 1, n1)), cti=f(-twi.T.reshape(n2, 1, n1)),
                br=f(br), bi=f(bi))


def _dft_s1_kernel(e_ref, top_ref, bot_ref, a_ref):
    rhs = jnp.concatenate([top_ref[...], bot_ref[...]], axis=0).astype(BF)
    res = jnp.dot(e_ref[...], rhs, preferred_element_type=F32)
    n1 = a_ref.shape[1]
    a_ref[0] = res[:n1].astype(a_ref.dtype)
    a_ref[1] = res[n1:].astype(a_ref.dtype)


def _dft_s1_call(e_mat, x3, top_idx, bot_idx, n1, tn=2048):
    h1, cols = x3.shape[1], x3.shape[2]
    return pl.pallas_call(
        _dft_s1_kernel, grid=(cols // tn,),
        in_specs=[pl.BlockSpec((2 * n1, 2 * h1), lambda j: (0, 0)),
                  pl.BlockSpec((None, h1, tn), lambda j: (top_idx, 0, j)),
                  pl.BlockSpec((None, h1, tn), lambda j: (bot_idx, 0, j))],
        out_specs=pl.BlockSpec((2, n1, tn), lambda j: (0, 0, j)),
        out_shape=jax.ShapeDtypeStruct((2, n1, cols), BF),
        compiler_params=_cparams("parallel"), name="hy_dft_stage1",
    )(e_mat.astype(BF), x3, x3)


def _twiddled_f2(f2r, f2i, tr, ti):
    gr = f2r * tr - f2i * ti
    gi = f2r * ti + f2i * tr
    return jnp.concatenate([jnp.concatenate([gr, -gi], axis=1), jnp.concatenate([gi, gr], axis=1)], axis=0)


def _dft_s2_filter_kernel(a_ref, f2r_ref, f2i_ref, twr_ref, twi_ref, kf_ref, *, kb):
    n2 = f2r_ref.shape[0]
    for j in range(kb):
        gm = _twiddled_f2(f2r_ref[...], f2i_ref[...], twr_ref[j], twi_ref[j]).astype(BF)
        rhs = jnp.concatenate([a_ref[0, j], a_ref[1, j]], axis=0)
        x = jnp.dot(gm, rhs, preferred_element_type=F32)
        kf_ref[0, j] = x[:n2]
        kf_ref[1, j] = x[n2:]


def _dft_s2_filter_call(a4, cst, kb=8):
    _, n1, n2, c = a4.shape
    blk = pl.BlockSpec((2, kb, n2, c), lambda i: (0, i, 0, 0))
    sq = pl.BlockSpec((n2, n2), lambda i: (0, 0))
    tw = pl.BlockSpec((kb, 1, n2), lambda i: (i, 0, 0))
    return pl.pallas_call(
        functools.partial(_dft_s2_filter_kernel, kb=kb), grid=(n1 // kb,),
        in_specs=[blk, sq, sq, tw, tw], out_specs=blk,
        out_shape=jax.ShapeDtypeStruct(a4.shape, F32),
        compiler_params=_cparams("parallel"), name="hy_dft_stage2_filter",
    )(a4, cst["f2r"], cst["f2i"], cst["twr"], cst["twi"])


def _dft_s2_conv_kernel(a_ref, kf_ref, f2r_ref, f2i_ref, dm_ref, twr_ref, twi_ref, b_ref, *, kb):
    n2 = f2r_ref.shape[0]
    dm = dm_ref[...]
    for j in range(kb):
        gm = _twiddled_f2(f2r_ref[...], f2i_ref[...], twr_ref[j], twi_ref[j]).astype(BF)
        rhs = jnp.concatenate([a_ref[0, j], a_ref[1, j]], axis=0)
        x = jnp.dot(gm, rhs, preferred_element_type=F32)
        xr, xi = x[:n2], x[n2:]
        kr, ki = kf_ref[0, j], kf_ref[1, j]
        y = jnp.concatenate([xr * kr - xi * ki, xr * ki + xi * kr], axis=0).astype(BF)
        bm = jnp.dot(dm, y, preferred_element_type=F32)
        b_ref[0, j] = bm[:n2].astype(b_ref.dtype)
        b_ref[1, j] = bm[n2:].astype(b_ref.dtype)


def _dft_s2_conv_call(a4, kf4, order_idx, cst, kb=8):
    _, n1, n2, c = a4.shape
    blk = pl.BlockSpec((2, kb, n2, c), lambda i: (0, i, 0, 0))
    kblk = pl.BlockSpec((None, 2, kb, n2, c), lambda i: (order_idx, 0, i, 0, 0))
    sq = pl.BlockSpec((n2, n2), lambda i: (0, 0))
    tw = pl.BlockSpec((kb, 1, n2), lambda i: (i, 0, 0))
    return pl.pallas_call(
        functools.partial(_dft_s2_conv_kernel, kb=kb), grid=(n1 // kb,),
        in_specs=[blk, kblk, sq, sq, pl.BlockSpec((2 * n2, 2 * n2), lambda i: (0, 0)), tw, tw],
        out_specs=blk, out_shape=jax.ShapeDtypeStruct(a4.shape, BF),
        compiler_params=_cparams("parallel"), name="hy_dft_stage2_conv",
    )(a4, kf4, cst["f2r"], cst["f2i"], cst["dm"].astype(BF), cst["twr"], cst["twi"])


def _dft_s1inv_kernel(b_ref, br_ref, bi_ref, ctr_ref, cti_ref, u_ref, gx_ref, skip_ref, o_ref, *, ng, c):
    h1 = br_ref.shape[0]
    skip = skip_ref[...]
    for j in range(ng):
        tr, ti = ctr_ref[j], cti_ref[j]
        mr = br_ref[...] * tr - bi_ref[...] * ti
        mi = br_ref[...] * ti + bi_ref[...] * tr
        em = jnp.concatenate([jnp.concatenate([mr, -mi], axis=1),
                              jnp.concatenate([mi, mr], axis=1)], axis=0).astype(BF)
        y = jnp.dot(em, b_ref[:, j * c:(j + 1) * c], preferred_element_type=F32)
        for b in range(2):
            u = u_ref[b, :, j * c:(j + 1) * c].astype(F32)
            g = gx_ref[b, :, j * c:(j + 1) * c].astype(F32)
            o_ref[b, :, j * c:(j + 1) * c] = (g * (y[b * h1:(b + 1) * h1] + skip * u)).astype(o_ref.dtype)


def _dft_s1inv_call(b2, u3, gx3, skip, cst, c, ng=4):
    rows, cols = b2.shape
    h1 = u3.shape[1]
    n2 = cols // c
    n1 = rows // 2
    ub = pl.BlockSpec((2, h1, ng * c), lambda i: (0, 0, i))
    cb = pl.BlockSpec((h1, n1), lambda i: (0, 0))
    tw = pl.BlockSpec((ng, 1, n1), lambda i: (i, 0, 0))
    return pl.pallas_call(
        functools.partial(_dft_s1inv_kernel, ng=ng, c=c), grid=(n2 // ng,),
        in_specs=[pl.BlockSpec((rows, ng * c), lambda i: (0, i)), cb, cb, tw, tw, ub, ub,
                  pl.BlockSpec((1, c), lambda i: (0, 0))],
        out_specs=ub, out_shape=jax.ShapeDtypeStruct(u3.shape, BF),
        compiler_params=_cparams("parallel"), name="hy_dft_stage1_inverse",
    )(b2, cst["br"], cst["bi"], cst["ctr"], cst["cti"], u3, gx3, skip.reshape(1, c))


def _filter_features(seq):
    t01 = jnp.linspace(0.0, 1.0, seq, dtype=F32)[:, None]
    w = (2.0 * math.pi) * jnp.arange(seq, dtype=F32)[:, None] / seq
    bands = jnp.linspace(1e-4, HY_BANDS - 1, HY_BANDS, dtype=F32)
    z = jnp.concatenate([t01, jnp.cos(bands * w), -jnp.sin(bands * w)], axis=-1)
    z2 = jnp.concatenate([z, z[:1], z[1:][::-1]], axis=0)
    return jnp.pad(z2, ((0, 0), (0, LANES - HY_EMB)))


def _hyena_filter_spectrum(seq, width, f1_w, f1_b, f2_w, f2_b, f3_w, freq, zfeat, cst):
    hid = f1_w.shape[1]
    n1 = 2 * seq // DFT_N2
    max_decay = math.log(HY_DECAY_TARGET) / HY_FAST_DECAY
    min_decay = math.log(HY_DECAY_TARGET) / HY_SLOW_DECAY
    adel = jnp.abs(jnp.linspace(min_decay, max_decay, width, dtype=F32)).reshape(1, width)
    f1w = jnp.pad(f1_w, ((0, LANES - HY_EMB), (0, 0)))
    f3d = f3_w.reshape(hid, HY_ORDER, 2, width).transpose(2, 0, 1, 3).reshape(2, hid, HY_ORDER * width)
    k2 = _filter_call(zfeat, f1w, f1_b.reshape(1, hid), f2_w, f2_b.reshape(-1, 1, hid), f3d,
                      freq.reshape(1, hid), adel, seq)
    kf = []
    for o in range(HY_ORDER):
        ko = k2[:, o * width:(o + 1) * width].reshape(2, n1 // 2, DFT_N2 * width)
        a = _dft_s1_call(cst["ef"], ko, 0, 1, n1)
        kf.append(_dft_s2_filter_call(a.reshape(2, n1, DFT_N2, width), cst))
    return jnp.stack(kf)


def _hyena_conv(u, gx, kf, order_idx, skip, cst, seq, width):
    n1 = 2 * seq // DFT_N2
    h1 = n1 // 2
    u3 = u.reshape(2, h1, DFT_N2 * width)
    a = _dft_s1_call(cst["e1"], u3, 0, 1, n1)
    bm = _dft_s2_conv_call(a.reshape(2, n1, DFT_N2, width), kf, order_idx, cst)
    y = _dft_s1inv_call(bm.reshape(2 * n1, DFT_N2 * width), u3, gx.reshape(2, h1, DFT_N2 * width),
                        skip, cst, width)
    return y.reshape(2 * seq, width)


def _rope_tables(seq):
    half = ROPE_DIM // 2
    inv = jnp.power(jnp.float32(ROPE_THETA), -jnp.arange(half, dtype=F32) * (2.0 / ROPE_DIM))
    ang = jnp.arange(seq, dtype=jnp.int32).astype(F32)[:, None] * inv
    cos, sin = jnp.cos(ang), jnp.sin(ang)
    rest = SW_HEAD_DIM - ROPE_DIM
    ones = jnp.ones((seq, rest), F32)
    zeros = jnp.zeros((seq, rest), F32)
    ch = jnp.concatenate([cos, cos, ones], axis=1)
    sh = jnp.concatenate([-sin, sin, zeros], axis=1)
    reps = LANES // SW_HEAD_DIM
    return jnp.tile(ch, (1, reps)), jnp.tile(sh, (1, reps))


def _rope_swap_matrix():
    half = ROPE_DIM // 2
    p = np.zeros((LANES, LANES), np.float32)
    for j in range(LANES):
        d = j % SW_HEAD_DIM
        if d < half:
            p[j + half, j] = 1.0
        elif d < ROPE_DIM:
            p[j - half, j] = 1.0
    return jnp.asarray(p)


def _rope_kernel(q_ref, k_ref, c_ref, s_ref, p_ref, qo_ref, ko_ref, *, scale):
    c, s, p = c_ref[...], s_ref[...], p_ref[...]

    def rot(x):
        sw = jnp.dot(x, p, preferred_element_type=F32)
        return x.astype(F32) * c + sw * s

    for g in range(q_ref.shape[1] // LANES):
        qo_ref[:, g * LANES:(g + 1) * LANES] = (rot(q_ref[:, g * LANES:(g + 1) * LANES]) * scale).astype(BF)
    for g in range(k_ref.shape[1] // LANES):
        ko_ref[:, g * LANES:(g + 1) * LANES] = rot(k_ref[:, g * LANES:(g + 1) * LANES]).astype(BF)


def _rope_call(proj, cos_t, sin_t, swap, batch, seq, q_blk, k_blk, qw, kw, tm=512):
    n = proj.shape[0]
    nt = seq // tm
    tab = pl.BlockSpec((tm, LANES), lambda b, i: (i, 0))
    return pl.pallas_call(
        functools.partial(_rope_kernel, scale=SW_HEAD_DIM ** -0.5), grid=(batch, nt),
        in_specs=[pl.BlockSpec((tm, qw), lambda b, i: (b * nt + i, q_blk)),
                  pl.BlockSpec((tm, kw), lambda b, i: (b * nt + i, k_blk)),
                  tab, tab, pl.BlockSpec((LANES, LANES), lambda b, i: (0, 0))],
        out_specs=[pl.BlockSpec((tm, qw), lambda b, i: (b * nt + i, 0)),
                   pl.BlockSpec((tm, kw), lambda b, i: (b * nt + i, 0))],
        out_shape=[jax.ShapeDtypeStruct((n, qw), BF), jax.ShapeDtypeStruct((n, kw), BF)],
        compiler_params=_cparams("parallel", "parallel"), name="sw_rope",
    )(proj, proj, cos_t, sin_t, swap.astype(BF))


def _attn_kernel(sink_ref, q_ref, kp_ref, km_ref, kn_ref, vp_ref, vm_ref, vn_ref, o_ref, *, tq, seq):
    blk = SW_WINDOW
    i = pl.program_id(1)
    kwin = jnp.concatenate([kp_ref[...], km_ref[...], kn_ref[...]], axis=0)
    vwin = jnp.concatenate([vp_ref[...], vm_ref[...], vn_ref[...]], axis=0)
    lo = lax.broadcasted_iota(jnp.int32, (tq + 2 * blk, LANES), 1) < SW_HEAD_DIM
    zero = jnp.zeros((), BF)
    kmask, vmask = [], []
    for h in range(SW_KV_HEADS):
        kh = kwin[:, h * LANES:(h + 1) * LANES]
        vh = vwin[:, h * LANES:(h + 1) * LANES]
        kmask.append((jnp.where(lo, kh, zero), jnp.where(lo, zero, kh)))
        vmask.append((jnp.where(lo, vh, zero), jnp.where(lo, zero, vh)))
    a_idx = lax.broadcasted_iota(jnp.int32, (blk, 3 * blk), 0)
    w_idx = lax.broadcasted_iota(jnp.int32, (blk, 3 * blk), 1)
    rel = w_idx - a_idx
    groups_per_kv = (SW_HEADS // SW_KV_HEADS) * SW_HEAD_DIM // LANES
    for jb in range(tq // blk):
        kpos = i * tq + (jb - 1) * blk + w_idx
        bad = (rel < 0) | (rel > 2 * SW_WINDOW) | (kpos < 0) | (kpos >= seq)
        for g in range(q_ref.shape[1] // LANES):
            h = g // groups_per_kv
            qg = q_ref[jb * blk:(jb + 1) * blk, g * LANES:(g + 1) * LANES]
            acc = jnp.zeros((blk, LANES), F32)
            for par in range(LANES // SW_HEAD_DIM):
                kx = kmask[h][par][jb * blk:(jb + 3) * blk]
                vx = vmask[h][par][jb * blk:(jb + 3) * blk]
                s = lax.dot_general(qg, kx, (((1,), (1,)), ((), ())), preferred_element_type=F32)
                s = jnp.where(bad, NEG_INF, s)
                sk = sink_ref[g * (LANES // SW_HEAD_DIM) + par]
                m = jnp.maximum(jnp.max(s, axis=-1, keepdims=True), sk)
                p = jnp.exp(s - m)
                denom = jnp.sum(p, axis=-1, keepdims=True) + jnp.exp(sk - m)
                acc = acc + jnp.dot(p.astype(BF), vx, preferred_element_type=F32) / denom
            o_ref[jb * blk:(jb + 1) * blk, g * LANES:(g + 1) * LANES] = acc.astype(BF)


def _attn_call(q_rot, k_rot, proj, v_blk, sink, batch, seq, tq=256):
    n, qw = q_rot.shape
    kw = k_rot.shape[1]
    blk = SW_WINDOW
    nt = seq // tq
    r = tq // blk
    nb = seq // blk
    main = lambda cb: pl.BlockSpec((tq, kw), lambda b, i, s: (b * nt + i, cb))
    prev = lambda cb: pl.BlockSpec((blk, kw), lambda b, i, s: (b * nb + jnp.maximum(i * r - 1, 0), cb))
    nxt = lambda cb: pl.BlockSpec((blk, kw), lambda b, i, s: (b * nb + jnp.minimum((i + 1) * r, nb - 1), cb))
    qspec = pl.BlockSpec((tq, qw), lambda b, i, s: (b * nt + i, 0))
    grid_spec = pltpu.PrefetchScalarGridSpec(
        num_scalar_prefetch=1, grid=(batch, nt),
        in_specs=[qspec, prev(0), main(0), nxt(0), prev(v_blk), main(v_blk), nxt(v_blk)],
        out_specs=qspec)
    return pl.pallas_call(
        functools.partial(_attn_kernel, tq=tq, seq=seq), grid_spec=grid_spec,
        out_shape=jax.ShapeDtypeStruct((n, qw), BF),
        compiler_params=_cparams("parallel", "parallel"), name="sw_attention",
    )(sink, q_rot, k_rot, k_rot, k_rot, proj, proj, proj)


def _memattn_kernel(q_ref, mk_ref, mv_ref, o_ref):
    scale = MEM_HEAD_DIM ** -0.5
    for h in range(MEM_HEADS):
        sl = slice(h * MEM_HEAD_DIM, (h + 1) * MEM_HEAD_DIM)
        s = lax.dot_general(q_ref[:, sl], mk_ref[:, sl], (((1,), (1,)), ((), ())),
                            preferred_element_type=F32) * scale
        m = jnp.max(s, axis=-1, keepdims=True)
        p = jnp.exp(s - m)
        denom = jnp.sum(p, axis=-1, keepdims=True)
        o = jnp.dot(p.astype(BF), mv_ref[:, sl], preferred_element_type=F32) / denom
        o_ref[:, sl] = o.astype(BF)


def _memattn_call(proj, q_blk, memkv, batch, seq, mem_tokens, tm=512):
    n = proj.shape[0]
    w = MEM_HEADS * MEM_HEAD_DIM
    nt = seq // tm
    return pl.pallas_call(
        _memattn_kernel, grid=(batch, nt),
        in_specs=[pl.BlockSpec((tm, w), lambda b, i: (b * nt + i, q_blk)),
                  pl.BlockSpec((mem_tokens, w), lambda b, i: (b, 0)),
                  pl.BlockSpec((mem_tokens, w), lambda b, i: (b, 1))],
        out_specs=pl.BlockSpec((tm, w), lambda b, i: (b * nt + i, 0)),
        out_shape=jax.ShapeDtypeStruct((n, w), BF),
        compiler_params=_cparams("parallel", "parallel"), name="mem_attention",
    )(proj, memkv, memkv)


def _merge_kernel(x_ref, gl_ref, bg_ref, yh_ref, ys_ref, ym_ref, wh_ref, ws_ref, wm_ref, wo_ref,
                  g_ref, b_ref, o_ref, ob_ref, *, alpha, d):
    merged = None
    for br, (y_ref, w_ref) in enumerate(((yh_ref, wh_ref), (ys_ref, ws_ref), (ym_ref, wm_ref))):
        gate = jax.nn.sigmoid(gl_ref[:, br * d:(br + 1) * d].astype(F32) + bg_ref[br:br + 1, :])
        t = gate * jnp.dot(y_ref[...], w_ref[...], preferred_element_type=F32)
        merged = t if merged is None else merged + t
    h = jnp.dot(merged.astype(BF), wo_ref[...], preferred_element_type=F32)
    y = _ln_rows(alpha * x_ref[...] + h, g_ref[...], b_ref[...])
    o_ref[...] = y
    ob_ref[...] = y.astype(BF)


def _merge_call(x, proj, gate_blk, b_gate, y_hy, y_sw, y_mem, w_hy, w_sw, w_mem, w_out, g, b, alpha, tm=256):
    n, d = x.shape
    row = lambda w: pl.BlockSpec((tm, w), lambda i: (i, 0))
    full = lambda a: pl.BlockSpec(a.shape, lambda i: (0,) * a.ndim)
    g2, b2 = g.reshape(1, d), b.reshape(1, d)
    return pl.pallas_call(
        functools.partial(_merge_kernel, alpha=alpha, d=d), grid=(n // tm,),
        in_specs=[row(d), pl.BlockSpec((tm, N_BRANCH * d), lambda i: (i, gate_blk)), full(b_gate),
                  row(y_hy.shape[1]), row(y_sw.shape[1]), row(y_mem.shape[1]),
                  full(w_hy), full(w_sw), full(w_mem), full(w_out), full(g2), full(b2)],
        out_specs=[row(d), row(d)],
        out_shape=[jax.ShapeDtypeStruct((n, d), F32), jax.ShapeDtypeStruct((n, d), BF)],
        compiler_params=_cparams("parallel"), name="merge_outproj_ln",
    )(x, proj, b_gate, y_hy, y_sw, y_mem, w_hy, w_sw, w_mem, w_out, g2, b2)


def _w1_prep_kernel(w_ref, p_ref, g_ref, l_ref):
    p = p_ref[...]
    for c in range(g_ref.shape[1] // LANES):
        r = jnp.dot(w_ref[:, 2 * c * LANES:2 * (c + 1) * LANES].astype(BF), p, preferred_element_type=F32)
        g_ref[:, c * LANES:(c + 1) * LANES] = r[:, :LANES].astype(BF)
        l_ref[:, c * LANES:(c + 1) * LANES] = r[:, LANES:].astype(BF)


def _w1_prep_call(w1, tr=512):
    depth, ne, d, de2 = w1.shape
    de = de2 // 2
    perm = np.zeros((2 * LANES, 2 * LANES), np.float32)
    perm[2 * np.arange(LANES), np.arange(LANES)] = 1.0
    perm[2 * np.arange(LANES) + 1, LANES + np.arange(LANES)] = 1.0
    out = pl.BlockSpec((None, None, tr, de), lambda l, e, i: (l, e, i, 0))
    return pl.pallas_call(
        _w1_prep_kernel, grid=(depth, ne, d // tr),
        in_specs=[pl.BlockSpec((None, None, tr, de2), lambda l, e, i: (l, e, i, 0)),
                  pl.BlockSpec((2 * LANES, 2 * LANES), lambda l, e, i: (0, 0))],
        out_specs=[out, out],
        out_shape=[jax.ShapeDtypeStruct((depth, ne, d, de), BF)] * 2,
        compiler_params=_cparams("parallel", "parallel", "parallel"), name="moe_w1_prep",
    )(w1, jnp.asarray(perm).astype(BF))


def _cast_kernel(x_ref, o_ref):
    o_ref[...] = x_ref[...].astype(o_ref.dtype)


def _w2_cast_call(w2, tr=512):
    depth, ne, de, d = w2.shape
    spec = pl.BlockSpec((None, None, tr, d), lambda l, e, i: (l, e, i, 0))
    return pl.pallas_call(
        _cast_kernel, grid=(depth, ne, de // tr), in_specs=[spec], out_specs=spec,
        out_shape=jax.ShapeDtypeStruct(w2.shape, BF),
        compiler_params=_cparams("parallel", "parallel", "parallel"), name="moe_w2_cast",
    )(w2)


def _expert_kernel(sg_ref, st_ref, lo_ref, hi_ref, x_ref, w1g_ref, w1l_ref, b1g_ref, b1l_ref, w2_ref, b2_ref,
                   y_ref):
    s = pl.program_id(0)
    lo, hi = lo_ref[s], hi_ref[s]

    @pl.when(hi > lo)
    def _():
        x = x_ref[...]
        hg = jnp.dot(x, w1g_ref[...], preferred_element_type=F32) + b1g_ref[...]
        hl = jnp.dot(x, w1l_ref[...], preferred_element_type=F32) + b1l_ref[...]
        hg = jnp.minimum(hg, SWIGLU_LIMIT)
        hl = jnp.clip(hl, -SWIGLU_LIMIT, SWIGLU_LIMIT)
        act = hg * jax.nn.sigmoid(SWIGLU_ALPHA * hg) * (hl + 1.0)
        y = (jnp.dot(act.astype(BF), w2_ref[...], preferred_element_type=F32) + b2_ref[...]).astype(y_ref.dtype)
        row = lax.broadcasted_iota(jnp.int32, y.shape, 0)
        mine = (row >= lo) & (row < hi)
        first_visit = (s == 0) | (st_ref[s] != st_ref[jnp.maximum(s - 1, 0)])

        @pl.when(first_visit)
        def _():
            y_ref[...] = jnp.where(mine, y, jnp.zeros_like(y))

        @pl.when(jnp.logical_not(first_visit))
        def _():
            y_ref[...] = jnp.where(mine, y, y_ref[...])


def _expert_call(steps, xs, w1g, w1l, b1g, b1l, w2, b2, layer):
    a, d = xs.shape
    de = w1g.shape[3]
    n_steps = steps[0].shape[0]
    wspec = lambda r, c: pl.BlockSpec((None, None, r, c), lambda i, sg, st, lo, hi: (layer, sg[i], 0, 0))
    rows = pl.BlockSpec((MOE_BLOCK, d), lambda i, sg, st, lo, hi: (st[i], 0))
    grid_spec = pltpu.PrefetchScalarGridSpec(
        num_scalar_prefetch=4, grid=(n_steps,),
        in_specs=[rows, wspec(d, de), wspec(d, de), wspec(1, de), wspec(1, de), wspec(de, d), wspec(1, d)],
        out_specs=rows)
    return pl.pallas_call(
        _expert_kernel, grid_spec=grid_spec, out_shape=jax.ShapeDtypeStruct((a, d), BF),
        compiler_params=_cparams("arbitrary"), name="moe_experts",
    )(*steps, xs, w1g, w1l, b1g, b1l, w2, b2)


def _route(x, router_w, router_b, n_experts):
    n = x.shape[0]
    logits = jnp.dot(x, router_w, precision=lax.Precision.HIGHEST) + router_b
    top_v, top_i = lax.top_k(logits, TOP_K)
    gate = jax.nn.softmax(top_v, axis=-1)
    a = n * TOP_K
    assert a % MOE_BLOCK == 0 and n_experts * a < 2 ** 31
    e_flat = top_i.reshape(-1).astype(jnp.int32)
    ids = jnp.arange(a, dtype=jnp.int32)
    skey = jnp.sort(e_flat * a + ids)
    asg = skey % a
    tok_sorted = asg // TOP_K
    _, pos = lax.sort((asg, ids), num_keys=1)
    experts = jnp.arange(n_experts, dtype=jnp.int32)
    counts = jnp.sum((e_flat[None, :] == experts[:, None]).astype(jnp.int32), axis=1)
    ends = jnp.cumsum(counts)
    starts = ends - counts
    n_tiles = a // MOE_BLOCK
    first_tile = starts // MOE_BLOCK
    ntile = jnp.where(counts > 0, (ends - 1) // MOE_BLOCK - first_tile + 1, 0)
    cum = jnp.cumsum(ntile)
    n_steps = n_tiles + n_experts - 1
    s = jnp.arange(n_steps, dtype=jnp.int32)
    g = jnp.minimum(jnp.sum((s[:, None] >= cum[None, :]).astype(jnp.int32), axis=1), n_experts - 1)
    live = s < cum[-1]
    tile = jnp.where(live, first_tile[g] + s - (cum[g] - ntile[g]), n_tiles - 1)
    lo = jnp.clip(starts[g] - tile * MOE_BLOCK, 0, MOE_BLOCK)
    hi = jnp.where(live, jnp.clip(ends[g] - tile * MOE_BLOCK, 0, MOE_BLOCK), lo)
    g = jnp.where(live, g, g[jnp.maximum(cum[-1] - 1, 0)])
    steps = tuple(v.astype(jnp.int32) for v in (g, tile, lo, hi))
    return gate, pos.reshape(n, TOP_K), tok_sorted, steps


def kernel(x, mem, ln_in_g, ln_in_b, w_in, b_gate, hy_conv_w, hy_conv_b, hy_f1_w, hy_f1_b, hy_f2_w, hy_f2_b, hy_f3_w, hy_freq, hy_skip, sw_sink, mem_w_kv, w_br_hy, w_br_swa, w_br_mem, w_out, ln1_g, ln1_b, router_w, router_b, moe_w1, moe_b1, moe_w2, moe_b2, ln2_g, ln2_b):
    batch, seq, d = x.shape
    depth = w_in.shape[0]
    n = batch * seq
    width = hy_skip.shape[2]
    hy_cols = 3 * width
    q_cols = SW_HEADS * SW_HEAD_DIM
    kv_cols = SW_KV_HEADS * SW_HEAD_DIM
    mq_cols = MEM_HEADS * MEM_HEAD_DIM
    gate_cols = N_BRANCH * d
    mem_tokens = mem.shape[1]
    n_experts = router_w.shape[2]
    alpha = (2 * depth) ** 0.25
    assert batch == 2, "the long convolution packs exactly two batch elements into one complex signal"
    assert w_in.shape[2] == hy_cols + q_cols + 2 * kv_cols + mq_cols + gate_cols
    n1 = 2 * seq // DFT_N2

    cst = _dft_consts(n1, DFT_N2)
    zfeat = _filter_features(seq)
    cos_t, sin_t = _rope_tables(seq)
    swap = _rope_swap_matrix()

    o_q = hy_cols
    o_k = o_q + q_cols
    o_v = o_k + kv_cols
    o_mq = o_v + kv_cols
    o_g = o_mq + mq_cols
    dup = lambda w0: jnp.concatenate(
        [w_in[:, :, w0 + h * SW_HEAD_DIM:w0 + (h + 1) * SW_HEAD_DIM] for h in range(SW_KV_HEADS) for _ in range(2)],
        axis=2)
    w_proj = jnp.concatenate([w_in[:, :, o_g:o_g + gate_cols], w_in[:, :, :hy_cols], w_in[:, :, o_q:o_q + q_cols],
                              w_in[:, :, o_mq:o_mq + mq_cols], dup(o_k), dup(o_v)], axis=2).astype(BF)
    kw = 2 * kv_cols
    c_hy, c_q, c_mq = gate_cols, gate_cols + hy_cols, gate_cols + hy_cols + q_cols
    c_k = c_mq + mq_cols
    c_v = c_k + kw
    proj_cols = c_v + kw
    tn_proj = proj_cols // 4

    w1g, w1l = _w1_prep_call(moe_w1)
    de = w1g.shape[-1]
    b1g = moe_b1[..., 0::2].reshape(depth, n_experts, 1, de)
    b1l = moe_b1[..., 1::2].reshape(depth, n_experts, 1, de)
    w2 = _w2_cast_call(moe_w2)
    b2 = moe_b2.reshape(depth, n_experts, 1, d)
    mem_bf = mem.reshape(batch * mem_tokens, d).astype(BF)

    xf, xb = _ln_call(x.reshape(n, d), ln_in_g, ln_in_b)
    for l in range(depth):
        proj = _mm_call(xb, w_proj[l], BF, 1024, tn_proj, "in_proj")

        hv, hx1, hx2 = _shortconv_call(proj, hy_conv_w[l], hy_conv_b[l], batch, seq, c_hy // hy_cols)
        kf = _hyena_filter_spectrum(seq, width, hy_f1_w[l], hy_f1_b[l], hy_f2_w[l], hy_f2_b[l],
                                    hy_f3_w[l], hy_freq[l], zfeat, cst)
        y_hy = hv
        for o, gx in enumerate((hx1, hx2)):
            y_hy = _hyena_conv(y_hy, gx, kf, o, hy_skip[l, o], cst, seq, width)

        q_rot, k_rot = _rope_call(proj, cos_t, sin_t, swap, batch, seq, c_q // q_cols, c_k // kw, q_cols, kw)
        y_sw = _attn_call(q_rot, k_rot, proj, c_v // kw, sw_sink[l], batch, seq)

        memkv = _mm_call(mem_bf, mem_w_kv[l].astype(BF), BF, mem_tokens, 2 * mq_cols, "mem_kv")
        y_mem = _memattn_call(proj, c_mq // mq_cols, memkv, batch, seq, mem_tokens)

        xf, xb = _merge_call(xf, proj, 0, b_gate[l], y_hy, y_sw, y_mem, w_br_hy[l].astype(BF),
                             w_br_swa[l].astype(BF), w_br_mem[l].astype(BF), w_out[l].astype(BF),
                             ln1_g[l], ln1_b[l], alpha)

        gate, dest, tok_sorted, steps = _route(xf, router_w[l], router_b[l], n_experts)
        xs = jnp.take(xb, tok_sorted, axis=0)
        ys = _expert_call(steps, xs, w1g, w1l, b1g, b1l, w2, b2, l)
        yg = jnp.take(ys, dest.T, axis=0)
        xf, xb = _combine_ln_call(xf, yg, gate, ln2_g[l], ln2_b[l], alpha)
    return xf.reshape(batch, seq, d)
```

```python
import functools
import math

import numpy as np
import jax
import jax.numpy as jnp
from jax import lax
from jax.experimental import pallas as pl
from jax.experimental.pallas import tpu as pltpu

BF = jnp.bfloat16
F32 = jnp.float32

HY_ORDER = 2
HY_SHORT = 3
HY_EMB = 33
HY_BANDS = (HY_EMB - 1) // 2
HY_FAST_DECAY = 0.3
HY_SLOW_DECAY = 1.5
HY_DECAY_TARGET = 1e-2
SW_HEADS = 8
SW_KV_HEADS = 2
SW_HEAD_DIM = 64
SW_WINDOW = 128
ROPE_THETA = 500000.0
ROPE_DIM = SW_HEAD_DIM // 4
MEM_HEADS = 4
MEM_HEAD_DIM = 128
N_BRANCH = 3
TOP_K = 4
SWIGLU_ALPHA = 1.702
SWIGLU_LIMIT = 7.0
MOE_BLOCK = 512
LN_EPS = 1e-5
NEG_INF = -1e30

LANES = 128
V7X_VMEM_BYTES = 64 * 1024 * 1024
VMEM_LIMIT = 48 * 1024 * 1024
DFT_N2 = 128
SUBLANES = 8
FEAT_ROWS = -(-HY_EMB // SUBLANES) * SUBLANES


def _cparams(*sem):
    return pltpu.CompilerParams(dimension_semantics=sem, vmem_limit_bytes=VMEM_LIMIT)


def _ln_rows(r, g, b):
    mu = jnp.mean(r, axis=-1, keepdims=True)
    d = r - mu
    var = jnp.mean(d * d, axis=-1, keepdims=True)
    return d * lax.rsqrt(var + LN_EPS) * g + b


def _ln_kernel(x_ref, g_ref, b_ref, y_ref, yb_ref):
    y = _ln_rows(x_ref[...], g_ref[...], b_ref[...])
    y_ref[...] = y
    yb_ref[...] = y.astype(BF)


def _ln_call(x, g, b, tm=512):
    n, d = x.shape
    row = pl.BlockSpec((tm, d), lambda i: (i, 0))
    vec = pl.BlockSpec((1, d), lambda i: (0, 0))
    return pl.pallas_call(
        _ln_kernel, grid=(n // tm,), in_specs=[row, vec, vec], out_specs=[row, row],
        out_shape=[jax.ShapeDtypeStruct((n, d), F32), jax.ShapeDtypeStruct((n, d), BF)],
        compiler_params=_cparams("parallel"), name="ln_entry",
    )(x, g.reshape(1, d), b.reshape(1, d))


def _combine_ln_kernel(x_ref, y_ref, w_ref, g_ref, b_ref, o_ref, ob_ref, *, alpha):
    w = w_ref[...]
    f = y_ref[0].astype(F32) * w[:, 0:1]
    for k in range(1, TOP_K):
        f = f + y_ref[k].astype(F32) * w[:, k:k + 1]
    y = _ln_rows(alpha * x_ref[...] + f, g_ref[...], b_ref[...])
    o_ref[...] = y
    ob_ref[...] = y.astype(BF)


def _combine_ln_call(x, yg, w, g, b, alpha, tm=512):
    n, d = x.shape
    row = pl.BlockSpec((tm, d), lambda i: (i, 0))
    vec = pl.BlockSpec((1, d), lambda i: (0, 0))
    return pl.pallas_call(
        functools.partial(_combine_ln_kernel, alpha=alpha), grid=(n // tm,),
        in_specs=[row, pl.BlockSpec((TOP_K, tm, d), lambda i: (0, i, 0)),
                  pl.BlockSpec((tm, TOP_K), lambda i: (i, 0)), vec, vec],
        out_specs=[row, row],
        out_shape=[jax.ShapeDtypeStruct((n, d), F32), jax.ShapeDtypeStruct((n, d), BF)],
        compiler_params=_cparams("parallel"), name="moe_combine_ln",
    )(x, yg, w, g.reshape(1, d), b.reshape(1, d))


def _mm_kernel(a_ref, b_ref, o_ref):
    o_ref[...] = jnp.dot(a_ref[...], b_ref[...], preferred_element_type=F32).astype(o_ref.dtype)


def _mm_call(a, b, out_dtype, tm, tn, name):
    m, k = a.shape
    n = b.shape[1]
    return pl.pallas_call(
        _mm_kernel, grid=(n // tn, m // tm),
        in_specs=[pl.BlockSpec((tm, k), lambda j, i: (i, 0)), pl.BlockSpec((k, tn), lambda j, i: (0, j))],
        out_specs=pl.BlockSpec((tm, tn), lambda j, i: (i, j)),
        out_shape=jax.ShapeDtypeStruct((m, n), out_dtype),
        compiler_params=_cparams("parallel", "parallel"), name=name,
    )(a, b)


def _shortconv_kernel(zp_ref, z_ref, zn_ref, w_ref, b_ref, hv_ref, hx1_ref, hx2_ref, *, tl, halo, width):
    i = pl.program_id(1)
    nt = pl.num_programs(1)
    z = z_ref[...].astype(F32)
    prev_row = zp_ref[halo - 1:halo, :].astype(F32)
    next_row = zn_ref[0:1, :].astype(F32)
    prev_row = jnp.where(i == 0, 0.0, prev_row)
    next_row = jnp.where(i == nt - 1, 0.0, next_row)
    row = lax.broadcasted_iota(jnp.int32, z.shape, 0)
    zm1 = jnp.where(row == 0, prev_row, pltpu.roll(z, 1, 0))
    zp1 = jnp.where(row == tl - 1, next_row, pltpu.roll(z, tl - 1, 0))
    w = w_ref[...]
    out = zm1 * w[0:1] + b_ref[...] + z * w[1:2] + zp1 * w[2:3]
    hv_ref[...] = out[:, 0:width].astype(BF)
    hx1_ref[...] = out[:, width:2 * width].astype(BF)
    hx2_ref[...] = out[:, 2 * width:3 * width].astype(BF)


def _shortconv_call(proj, conv_w, conv_b, batch, seq, col_blk, tl=512, halo=16):
    n = proj.shape[0]
    c3 = conv_w.shape[1]
    width = c3 // 3
    nt = seq // tl
    hb = tl // halo
    nhb = seq // halo
    main = pl.BlockSpec((tl, c3), lambda b, i: (b * nt + i, col_blk))
    prev = pl.BlockSpec((halo, c3), lambda b, i: (b * nhb + jnp.maximum(i * hb - 1, 0), col_blk))
    nxt = pl.BlockSpec((halo, c3), lambda b, i: (b * nhb + jnp.minimum((i + 1) * hb, nhb - 1), col_blk))
    out = pl.BlockSpec((tl, width), lambda b, i: (b * nt + i, 0))
    return pl.pallas_call(
        functools.partial(_shortconv_kernel, tl=tl, halo=halo, width=width), grid=(batch, nt),
        in_specs=[prev, main, nxt, pl.BlockSpec((HY_SHORT, c3), lambda b, i: (0, 0)),
                  pl.BlockSpec((1, c3), lambda b, i: (0, 0))],
        out_specs=[out, out, out],
        out_shape=[jax.ShapeDtypeStruct((n, width), BF)] * 3,
        compiler_params=_cparams("parallel", "parallel"), name="hy_shortconv",
    )(proj, proj, proj, conv_w, conv_b.reshape(1, c3))


def _filter_kernel(z_ref, f1w_ref, f1b_ref, f2w_ref, f2b_ref, f3w_ref, freq_ref, adel_ref, k_ref,
                   *, tr, seq, n_inner):
    i = pl.program_id(0)
    hp = lax.Precision.HIGHEST
    fr = freq_ref[...]
    h = jnp.sin(fr * (jnp.dot(f1w_ref[...], z_ref[...], precision=hp, preferred_element_type=F32)
                      + f1b_ref[...]))
    for j in range(n_inner):
        h = jnp.sin(fr * (jnp.dot(f2w_ref[j], h, precision=hp, preferred_element_type=F32) + f2b_ref[j]))
    k = lax.dot_general(h.astype(BF), f3w_ref[...], (((0,), (0,)), ((), ())),
                        preferred_element_type=F32)
    width = adel_ref.shape[1]
    row = i * tr + lax.broadcasted_iota(jnp.int32, (tr, width), 0)
    pos = jnp.where(row < seq, row, 2 * seq - row)
    win = jnp.exp(-(pos.astype(F32) * (1.0 / (seq - 1))) * adel_ref[...])
    win = jnp.where(row == seq, 0.0, win)
    k_ref[...] = k * jnp.concatenate([win] * HY_ORDER, axis=1)


def _filter_call(zfeat_t, f1w_t, f1b, f2w_t, f2b, f3w_dir, freq, adel, seq, tr=1024):
    zr, n2l = zfeat_t.shape
    hid = f1w_t.shape[0]
    n_inner = f2w_t.shape[0]
    ow = f3w_dir.shape[2]
    width = adel.shape[1]
    half = (n2l // tr) // 2
    full = lambda *shape: pl.BlockSpec(shape, lambda i: (0,) * len(shape))
    return pl.pallas_call(
        functools.partial(_filter_kernel, tr=tr, seq=seq, n_inner=n_inner), grid=(n2l // tr,),
        in_specs=[pl.BlockSpec((zr, tr), lambda i: (0, i)), full(hid, zr), full(hid, 1),
                  full(n_inner, hid, hid), full(n_inner, hid, 1),
                  pl.BlockSpec((None, hid, ow), lambda i: (i // half, 0, 0)),
                  full(hid, 1), full(1, width)],
        out_specs=pl.BlockSpec((tr, ow), lambda i: (i, 0)),
        out_shape=jax.ShapeDtypeStruct((n2l, ow), F32),
        compiler_params=_cparams("parallel"), name="hy_filter",
    )(zfeat_t, f1w_t, f1b, f2w_t, f2b, f3w_dir, freq, adel)


def _dft_consts(n1, n2):
    n = n1 * n2
    h1 = n1 // 2
    k1 = np.arange(n1)[:, None]
    ang = -2.0 * np.pi * (k1 * np.arange(h1)[None, :]) / n1
    cr, ci = np.cos(ang), np.sin(ang)
    e1 = np.block([[cr, -ci], [ci, cr]])
    angf = -2.0 * np.pi * (k1 * np.arange(n1)[None, :]) / n1
    ef = np.concatenate([np.cos(angf), np.sin(angf)], axis=0)
    a2 = -2.0 * np.pi * (np.arange(n2)[:, None] * np.arange(n2)[None, :]) / n2
    f2r, f2i = np.cos(a2), np.sin(a2)
    dm = np.block([[f2r, f2i], [-f2i, f2r]])
    at = -2.0 * np.pi * (np.arange(n1)[:, None] * np.arange(n2)[None, :]) / n
    twr, twi = np.cos(at), np.sin(at)
    ab = 2.0 * np.pi * (np.arange(h1)[:, None] * np.arange(n1)[None, :]) / n1
    br, bi = np.cos(ab) / n, np.sin(ab) / n
    f = lambda a: jnp.asarray(a, dtype=F32)
    return dict(e1=f(e1), ef=f(ef), f2r=f(f2r), f2i=f(f2i), dm=f(dm),
                twr=f(twr.reshape(n1, 1, n2)), twi=f(twi.reshape(n1, 1, n2)),
                ctr=f(twr.T.reshape(n2, 1, n1)), cti=f(-twi.T.reshape(n2, 1, n1)),
                br=f(br), bi=f(bi))


def _dft_s1_kernel(e_ref, top_ref, bot_ref, a_ref):
    rhs = jnp.concatenate([top_ref[...], bot_ref[...]], axis=0).astype(BF)
    res = jnp.dot(e_ref[...], rhs, preferred_element_type=F32)
    n1 = a_ref.shape[1]
    a_ref[0] = res[:n1].astype(a_ref.dtype)
    a_ref[1] = res[n1:].astype(a_ref.dtype)


def _dft_s1_call(e_mat, x3, top_idx, bot_idx, n1, tn=2048):
    h1, cols = x3.shape[1], x3.shape[2]
    return pl.pallas_call(
        _dft_s1_kernel, grid=(cols // tn,),
        in_specs=[pl.BlockSpec((2 * n1, 2 * h1), lambda j: (0, 0)),
                  pl.BlockSpec((None, h1, tn), lambda j: (top_idx, 0, j)),
                  pl.BlockSpec((None, h1, tn), lambda j: (bot_idx, 0, j))],
        out_specs=pl.BlockSpec((2, n1, tn), lambda j: (0, 0, j)),
        out_shape=jax.ShapeDtypeStruct((2, n1, cols), BF),
        compiler_params=_cparams("parallel"), name="hy_dft_stage1",
    )(e_mat.astype(BF), x3, x3)


def _twiddled_f2(f2r, f2i, tr, ti):
    gr = f2r * tr - f2i * ti
    gi = f2r * ti + f2i * tr
    return jnp.concatenate([jnp.concatenate([gr, -gi], axis=1), jnp.concatenate([gi, gr], axis=1)], axis=0)


def _dft_s2_conv_kernel(a_ref, af_ref, f2r_ref, f2i_ref, dm_ref, twr_ref, twi_ref, b_ref, *, kb):
    n2 = f2r_ref.shape[0]
    dm = dm_ref[...]
    for j in range(kb):
        gm = _twiddled_f2(f2r_ref[...], f2i_ref[...], twr_ref[j], twi_ref[j]).astype(BF)
        rhs = jnp.concatenate([a_ref[0, j], a_ref[1, j]], axis=0)
        x = jnp.dot(gm, rhs, preferred_element_type=F32)
        kf = jnp.dot(gm, jnp.concatenate([af_ref[0, j], af_ref[1, j]], axis=0),
                     preferred_element_type=F32)
        xr, xi = x[:n2], x[n2:]
        kr, ki = kf[:n2], kf[n2:]
        y = jnp.concatenate([xr * kr - xi * ki, xr * ki + xi * kr], axis=0).astype(BF)
        bm = jnp.dot(dm, y, preferred_element_type=F32)
        b_ref[0, j] = bm[:n2].astype(b_ref.dtype)
        b_ref[1, j] = bm[n2:].astype(b_ref.dtype)


def _dft_s2_conv_call(a4, af4, cst, kb=8):
    _, n1, n2, c = a4.shape
    blk = pl.BlockSpec((2, kb, n2, c), lambda i: (0, i, 0, 0))
    sq = pl.BlockSpec((n2, n2), lambda i: (0, 0))
    tw = pl.BlockSpec((kb, 1, n2), lambda i: (i, 0, 0))
    return pl.pallas_call(
        functools.partial(_dft_s2_conv_kernel, kb=kb), grid=(n1 // kb,),
        in_specs=[blk, blk, sq, sq, pl.BlockSpec((2 * n2, 2 * n2), lambda i: (0, 0)), tw, tw],
        out_specs=blk, out_shape=jax.ShapeDtypeStruct(a4.shape, BF),
        compiler_params=_cparams("parallel"), name="hy_dft_stage2_conv",
    )(a4, af4, cst["f2r"], cst["f2i"], cst["dm"].astype(BF), cst["twr"], cst["twi"])


def _dft_s1inv_kernel(b_ref, br_ref, bi_ref, ctr_ref, cti_ref, u_ref, gx_ref, skip_ref, o_ref, *, ng, c):
    h1 = br_ref.shape[0]
    skip = skip_ref[...]
    for j in range(ng):
        tr, ti = ctr_ref[j], cti_ref[j]
        mr = br_ref[...] * tr - bi_ref[...] * ti
        mi = br_ref[...] * ti + bi_ref[...] * tr
        em = jnp.concatenate([jnp.concatenate([mr, -mi], axis=1),
                              jnp.concatenate([mi, mr], axis=1)], axis=0).astype(BF)
        y = jnp.dot(em, b_ref[:, j * c:(j + 1) * c], preferred_element_type=F32)
        for b in range(2):
            u = u_ref[b, :, j * c:(j + 1) * c].astype(F32)
            g = gx_ref[b, :, j * c:(j + 1) * c].astype(F32)
            o_ref[b, :, j * c:(j + 1) * c] = (g * (y[b * h1:(b + 1) * h1] + skip * u)).astype(o_ref.dtype)


def _dft_s1inv_call(b2, u3, gx3, skip, cst, c, ng=4):
    rows, cols = b2.shape
    h1 = u3.shape[1]
    n2 = cols // c
    n1 = rows // 2
    ub = pl.BlockSpec((2, h1, ng * c), lambda i: (0, 0, i))
    cb = pl.BlockSpec((h1, n1), lambda i: (0, 0))
    tw = pl.BlockSpec((ng, 1, n1), lambda i: (i, 0, 0))
    return pl.pallas_call(
        functools.partial(_dft_s1inv_kernel, ng=ng, c=c), grid=(n2 // ng,),
        in_specs=[pl.BlockSpec((rows, ng * c), lambda i: (0, i)), cb, cb, tw, tw, ub, ub,
                  pl.BlockSpec((1, c), lambda i: (0, 0))],
        out_specs=ub, out_shape=jax.ShapeDtypeStruct(u3.shape, BF),
        compiler_params=_cparams("parallel"), name="hy_dft_stage1_inverse",
    )(b2, cst["br"], cst["bi"], cst["ctr"], cst["cti"], u3, gx3, skip.reshape(1, c))


def _filter_features(seq):
    t01 = jnp.linspace(0.0, 1.0, seq, dtype=F32)[:, None]
    w = (2.0 * math.pi) * jnp.arange(seq, dtype=F32)[:, None] / seq
    bands = jnp.linspace(1e-4, HY_BANDS - 1, HY_BANDS, dtype=F32)
    z = jnp.concatenate([t01, jnp.cos(bands * w), -jnp.sin(bands * w)], axis=-1)
    z2 = jnp.concatenate([z, z[:1], z[1:][::-1]], axis=0)
    return jnp.pad(z2, ((0, 0), (0, FEAT_ROWS - HY_EMB))).T


def _hyena_filter_stage1(seq, width, f1_w, f1_b, f2_w, f2_b, f3_w, freq, zfeat_t, cst):
    hid = f1_w.shape[1]
    n1 = 2 * seq // DFT_N2
    max_decay = math.log(HY_DECAY_TARGET) / HY_FAST_DECAY
    min_decay = math.log(HY_DECAY_TARGET) / HY_SLOW_DECAY
    adel = jnp.abs(jnp.linspace(min_decay, max_decay, width, dtype=F32)).reshape(1, width)
    f1w_t = jnp.pad(f1_w, ((0, FEAT_ROWS - HY_EMB), (0, 0))).T
    f3d = f3_w.reshape(hid, HY_ORDER, 2, width).transpose(2, 0, 1, 3).reshape(2, hid, HY_ORDER * width)
    k2 = _filter_call(zfeat_t, f1w_t, f1_b.reshape(hid, 1), f2_w.transpose(0, 2, 1), f2_b.reshape(-1, hid, 1),
                      f3d.astype(BF), freq.reshape(hid, 1), adel, seq)
    af = []
    for o in range(HY_ORDER):
        ko = k2[:, o * width:(o + 1) * width].reshape(2, n1 // 2, DFT_N2 * width)
        af.append(_dft_s1_call(cst["ef"], ko, 0, 1, n1).reshape(2, n1, DFT_N2, width))
    return af


def _hyena_conv(u, gx, af, skip, cst, seq, width):
    n1 = 2 * seq // DFT_N2
    h1 = n1 // 2
    u3 = u.reshape(2, h1, DFT_N2 * width)
    a = _dft_s1_call(cst["e1"], u3, 0, 1, n1)
    bm = _dft_s2_conv_call(a.reshape(2, n1, DFT_N2, width), af, cst)
    y = _dft_s1inv_call(bm.reshape(2 * n1, DFT_N2 * width), u3, gx.reshape(2, h1, DFT_N2 * width),
                        skip, cst, width)
    return y.reshape(2 * seq, width)


def _rope_tables(seq):
    half = ROPE_DIM // 2
    inv = jnp.power(jnp.float32(ROPE_THETA), -jnp.arange(half, dtype=F32) * (2.0 / ROPE_DIM))
    ang = jnp.arange(seq, dtype=jnp.int32).astype(F32)[:, None] * inv
    cos, sin = jnp.cos(ang), jnp.sin(ang)
    rest = SW_HEAD_DIM - ROPE_DIM
    ones = jnp.ones((seq, rest), F32)
    zeros = jnp.zeros((seq, rest), F32)
    ch = jnp.concatenate([cos, cos, ones], axis=1)
    sh = jnp.concatenate([-sin, sin, zeros], axis=1)
    reps = LANES // SW_HEAD_DIM
    return jnp.tile(ch, (1, reps)), jnp.tile(sh, (1, reps))


def _rope_swap_matrix():
    half = ROPE_DIM // 2
    p = np.zeros((LANES, LANES), np.float32)
    for j in range(LANES):
        d = j % SW_HEAD_DIM
        if d < half:
            p[j + half, j] = 1.0
        elif d < ROPE_DIM:
            p[j - half, j] = 1.0
    return jnp.asarray(p)


def _rope_kernel(q_ref, k_ref, c_ref, s_ref, p_ref, qo_ref, ko_ref, *, scale):
    c, s, p = c_ref[...], s_ref[...], p_ref[...]

    def rot(x):
        sw = jnp.dot(x, p, preferred_element_type=F32)
        return x.astype(F32) * c + sw * s

    for g in range(q_ref.shape[1] // LANES):
        qo_ref[:, g * LANES:(g + 1) * LANES] = (rot(q_ref[:, g * LANES:(g + 1) * LANES]) * scale).astype(BF)
    for g in range(k_ref.shape[1] // LANES):
        ko_ref[:, g * LANES:(g + 1) * LANES] = rot(k_ref[:, g * LANES:(g + 1) * LANES]).astype(BF)


def _rope_call(proj, cos_t, sin_t, swap, batch, seq, q_blk, k_blk, qw, kw, tm=512):
    n = proj.shape[0]
    nt = seq // tm
    tab = pl.BlockSpec((tm, LANES), lambda b, i: (i, 0))
    return pl.pallas_call(
        functools.partial(_rope_kernel, scale=SW_HEAD_DIM ** -0.5), grid=(batch, nt),
        in_specs=[pl.BlockSpec((tm, qw), lambda b, i: (b * nt + i, q_blk)),
                  pl.BlockSpec((tm, kw), lambda b, i: (b * nt + i, k_blk)),
                  tab, tab, pl.BlockSpec((LANES, LANES), lambda b, i: (0, 0))],
        out_specs=[pl.BlockSpec((tm, qw), lambda b, i: (b * nt + i, 0)),
                   pl.BlockSpec((tm, kw), lambda b, i: (b * nt + i, 0))],
        out_shape=[jax.ShapeDtypeStruct((n, qw), BF), jax.ShapeDtypeStruct((n, kw), BF)],
        compiler_params=_cparams("parallel", "parallel"), name="sw_rope",
    )(proj, proj, cos_t, sin_t, swap.astype(BF))


def _attn_kernel(sink_ref, q_ref, kp_ref, km_ref, kn_ref, vp_ref, vm_ref, vn_ref, o_ref, *, tq, seq):
    blk = SW_WINDOW
    i = pl.program_id(1)
    kwin = jnp.concatenate([kp_ref[...], km_ref[...], kn_ref[...]], axis=0)
    vwin_t = jnp.concatenate([vp_ref[...], vm_ref[...], vn_ref[...]], axis=1)
    lo = lax.broadcasted_iota(jnp.int32, (blk, LANES), 1) < SW_HEAD_DIM
    zero = jnp.zeros((), BF)
    per_group = LANES // SW_HEAD_DIM
    groups_per_kv = (SW_HEADS // SW_KV_HEADS) // per_group
    stack = groups_per_kv * per_group
    w_idx = lax.broadcasted_iota(jnp.int32, (3 * blk, stack * blk), 0)
    a_idx = lax.broadcasted_iota(jnp.int32, (3 * blk, stack * blk), 1) & (blk - 1)
    rel = w_idx - a_idx
    for jb in range(tq // blk):
        kpos = i * tq + (jb - 1) * blk + w_idx
        bad = (rel < 0) | (rel > 2 * SW_WINDOW) | (kpos < 0) | (kpos >= seq)
        for h in range(SW_KV_HEADS):
            parts, sinks = [], []
            for gg in range(groups_per_kv):
                g = h * groups_per_kv + gg
                qg = q_ref[jb * blk:(jb + 1) * blk, g * LANES:(g + 1) * LANES]
                parts += [jnp.where(lo, qg, zero), jnp.where(lo, zero, qg)]
                sinks += [jnp.full((1, blk), sink_ref[g * per_group + par], F32) for par in range(per_group)]
            qs = jnp.concatenate(parts, axis=0)
            sk = jnp.concatenate(sinks, axis=1)
            kh = kwin[jb * blk:(jb + 3) * blk, h * LANES:(h + 1) * LANES]
            vt = vwin_t[h * LANES:(h + 1) * LANES, jb * blk:(jb + 3) * blk]
            s = lax.dot_general(kh, qs, (((1,), (1,)), ((), ())), preferred_element_type=F32)
            s = jnp.where(bad, NEG_INF, s)
            m = jnp.maximum(jnp.max(s, axis=0, keepdims=True), sk)
            p = jnp.exp(s - m)
            denom = jnp.sum(p, axis=0, keepdims=True) + jnp.exp(sk - m)
            ot = jnp.dot(vt, p.astype(BF), preferred_element_type=F32) / denom
            o = ot.T
            for gg in range(groups_per_kv):
                g = h * groups_per_kv + gg
                oa = o[(gg * per_group) * blk:(gg * per_group + 1) * blk]
                ob = o[(gg * per_group + 1) * blk:(gg * per_group + 2) * blk]
                o_ref[jb * blk:(jb + 1) * blk, g * LANES:(g + 1) * LANES] = jnp.where(lo, oa, ob).astype(BF)


def _attn_call(q_rot, k_rot, v_t, sink, batch, seq, tq=256):
    n, qw = q_rot.shape
    kw = k_rot.shape[1]
    blk = SW_WINDOW
    nt = seq // tq
    r = tq // blk
    nb = seq // blk
    prev_i = lambda b, i: b * nb + jnp.maximum(i * r - 1, 0)
    next_i = lambda b, i: b * nb + jnp.minimum((i + 1) * r, nb - 1)
    qspec = pl.BlockSpec((tq, qw), lambda b, i, s: (b * nt + i, 0))
    grid_spec = pltpu.PrefetchScalarGridSpec(
        num_scalar_prefetch=1, grid=(batch, nt),
        in_specs=[qspec,
                  pl.BlockSpec((blk, kw), lambda b, i, s: (prev_i(b, i), 0)),
                  pl.BlockSpec((tq, kw), lambda b, i, s: (b * nt + i, 0)),
                  pl.BlockSpec((blk, kw), lambda b, i, s: (next_i(b, i), 0)),
                  pl.BlockSpec((kw, blk), lambda b, i, s: (0, prev_i(b, i))),
                  pl.BlockSpec((kw, tq), lambda b, i, s: (0, b * nt + i)),
                  pl.BlockSpec((kw, blk), lambda b, i, s: (0, next_i(b, i)))],
        out_specs=qspec)
    return pl.pallas_call(
        functools.partial(_attn_kernel, tq=tq, seq=seq), grid_spec=grid_spec,
        out_shape=jax.ShapeDtypeStruct((n, qw), BF),
        compiler_params=_cparams("parallel", "parallel"), name="sw_attention",
    )(sink, q_rot, k_rot, k_rot, k_rot, v_t, v_t, v_t)


def _memattn_kernel(q_ref, mk_ref, mv_ref, o_ref):
    scale = MEM_HEAD_DIM ** -0.5
    for h in range(MEM_HEADS):
        sl = slice(h * MEM_HEAD_DIM, (h + 1) * MEM_HEAD_DIM)
        s = lax.dot_general(q_ref[:, sl], mk_ref[:, sl], (((1,), (1,)), ((), ())),
                            preferred_element_type=F32) * scale
        m = jnp.max(s, axis=-1, keepdims=True)
        p = jnp.exp(s - m)
        denom = jnp.sum(p, axis=-1, keepdims=True)
        o = jnp.dot(p.astype(BF), mv_ref[:, sl], preferred_element_type=F32) / denom
        o_ref[:, sl] = o.astype(BF)


def _memattn_call(proj, q_blk, memkv, batch, seq, mem_tokens, tm=512):
    n = proj.shape[0]
    w = MEM_HEADS * MEM_HEAD_DIM
    nt = seq // tm
    return pl.pallas_call(
        _memattn_kernel, grid=(batch, nt),
        in_specs=[pl.BlockSpec((tm, w), lambda b, i: (b * nt + i, q_blk)),
                  pl.BlockSpec((mem_tokens, w), lambda b, i: (b, 0)),
                  pl.BlockSpec((mem_tokens, w), lambda b, i: (b, 1))],
        out_specs=pl.BlockSpec((tm, w), lambda b, i: (b * nt + i, 0)),
        out_shape=jax.ShapeDtypeStruct((n, w), BF),
        compiler_params=_cparams("parallel", "parallel"), name="mem_attention",
    )(proj, memkv, memkv)


def _merge_kernel(x_ref, gl_ref, bg_ref, yh_ref, ys_ref, ym_ref, wh_ref, ws_ref, wm_ref, wo_ref,
                  g_ref, b_ref, o_ref, ob_ref, *, alpha, d):
    merged = None
    for br, (y_ref, w_ref) in enumerate(((yh_ref, wh_ref), (ys_ref, ws_ref), (ym_ref, wm_ref))):
        gate = jax.nn.sigmoid(gl_ref[:, br * d:(br + 1) * d].astype(F32) + bg_ref[br:br + 1, :])
        t = gate * jnp.dot(y_ref[...], w_ref[...], preferred_element_type=F32)
        merged = t if merged is None else merged + t
    h = jnp.dot(merged.astype(BF), wo_ref[...], preferred_element_type=F32)
    y = _ln_rows(alpha * x_ref[...] + h, g_ref[...], b_ref[...])
    o_ref[...] = y
    ob_ref[...] = y.astype(BF)


def _merge_call(x, proj, gate_blk, b_gate, y_hy, y_sw, y_mem, w_hy, w_sw, w_mem, w_out, g, b, alpha, tm=512):
    n, d = x.shape
    row = lambda w: pl.BlockSpec((tm, w), lambda i: (i, 0))
    full = lambda a: pl.BlockSpec(a.shape, lambda i: (0,) * a.ndim)
    g2, b2 = g.reshape(1, d), b.reshape(1, d)
    return pl.pallas_call(
        functools.partial(_merge_kernel, alpha=alpha, d=d), grid=(n // tm,),
        in_specs=[row(d), pl.BlockSpec((tm, N_BRANCH * d), lambda i: (i, gate_blk)), full(b_gate),
                  row(y_hy.shape[1]), row(y_sw.shape[1]), row(y_mem.shape[1]),
                  full(w_hy), full(w_sw), full(w_mem), full(w_out), full(g2), full(b2)],
        out_specs=[row(d), row(d)],
        out_shape=[jax.ShapeDtypeStruct((n, d), F32), jax.ShapeDtypeStruct((n, d), BF)],
        compiler_params=_cparams("parallel"), name="merge_outproj_ln",
    )(x, proj, b_gate, y_hy, y_sw, y_mem, w_hy, w_sw, w_mem, w_out, g2, b2)


def _w1_prep_kernel(w_ref, p_ref, g_ref, l_ref):
    p = p_ref[...]
    for c in range(g_ref.shape[1] // LANES):
        r = jnp.dot(w_ref[:, 2 * c * LANES:2 * (c + 1) * LANES].astype(BF), p, preferred_element_type=F32)
        g_ref[:, c * LANES:(c + 1) * LANES] = r[:, :LANES].astype(BF)
        l_ref[:, c * LANES:(c + 1) * LANES] = r[:, LANES:].astype(BF)


def _w1_prep_call(w1, tr=512):
    depth, ne, d, de2 = w1.shape
    de = de2 // 2
    perm = np.zeros((2 * LANES, 2 * LANES), np.float32)
    perm[2 * np.arange(LANES), np.arange(LANES)] = 1.0
    perm[2 * np.arange(LANES) + 1, LANES + np.arange(LANES)] = 1.0
    out = pl.BlockSpec((None, None, tr, de), lambda l, e, i: (l, e, i, 0))
    return pl.pallas_call(
        _w1_prep_kernel, grid=(depth, ne, d // tr),
        in_specs=[pl.BlockSpec((None, None, tr, de2), lambda l, e, i: (l, e, i, 0)),
                  pl.BlockSpec((2 * LANES, 2 * LANES), lambda l, e, i: (0, 0))],
        out_specs=[out, out],
        out_shape=[jax.ShapeDtypeStruct((depth, ne, d, de), BF)] * 2,
        compiler_params=_cparams("parallel", "parallel", "parallel"), name="moe_w1_prep",
    )(w1, jnp.asarray(perm).astype(BF))


def _cast_kernel(x_ref, o_ref):
    o_ref[...] = x_ref[...].astype(o_ref.dtype)


def _w2_cast_call(w2, tr=512):
    depth, ne, de, d = w2.shape
    spec = pl.BlockSpec((None, None, tr, d), lambda l, e, i: (l, e, i, 0))
    return pl.pallas_call(
        _cast_kernel, grid=(depth, ne, de // tr), in_specs=[spec], out_specs=spec,
        out_shape=jax.ShapeDtypeStruct(w2.shape, BF),
        compiler_params=_cparams("parallel", "parallel", "parallel"), name="moe_w2_cast",
    )(w2)


def _expert_kernel(sg_ref, st_ref, lo_ref, hi_ref, x_ref, w1g_ref, w1l_ref, b1g_ref, b1l_ref, w2_ref, b2_ref,
                   y_ref):
    s = pl.program_id(0)
    lo, hi = lo_ref[s], hi_ref[s]

    @pl.when(hi > lo)
    def _():
        x = x_ref[...]
        hg = jnp.dot(x, w1g_ref[...], preferred_element_type=F32) + b1g_ref[...]
        hl = jnp.dot(x, w1l_ref[...], preferred_element_type=F32) + b1l_ref[...]
        hg = jnp.minimum(hg, SWIGLU_LIMIT)
        hl = jnp.clip(hl, -SWIGLU_LIMIT, SWIGLU_LIMIT)
        act = hg * jax.nn.sigmoid(SWIGLU_ALPHA * hg) * (hl + 1.0)
        y = jnp.dot(act.astype(BF), w2_ref[...], preferred_element_type=F32) + b2_ref[...]
        whole = (lo == 0) & (hi == y.shape[0])
        first_visit = (s == 0) | (st_ref[s] != st_ref[jnp.maximum(s - 1, 0)])

        @pl.when(whole)
        def _():
            y_ref[...] = y.astype(y_ref.dtype)

        def masked(prev):
            row = lax.broadcasted_iota(jnp.int32, y.shape, 0)
            y_ref[...] = jnp.where((row >= lo) & (row < hi), y, prev).astype(y_ref.dtype)

        @pl.when(jnp.logical_not(whole) & first_visit)
        def _():
            masked(jnp.zeros_like(y))

        @pl.when(jnp.logical_not(whole) & jnp.logical_not(first_visit))
        def _():
            masked(y_ref[...].astype(F32))


def _expert_call(steps, xs, w1g, w1l, b1g, b1l, w2, b2, layer):
    a, d = xs.shape
    de = w1g.shape[3]
    n_steps = steps[0].shape[0]
    wspec = lambda r, c: pl.BlockSpec((None, None, r, c), lambda i, sg, st, lo, hi: (layer, sg[i], 0, 0))
    rows = pl.BlockSpec((MOE_BLOCK, d), lambda i, sg, st, lo, hi: (st[i], 0))
    grid_spec = pltpu.PrefetchScalarGridSpec(
        num_scalar_prefetch=4, grid=(n_steps,),
        in_specs=[rows, wspec(d, de), wspec(d, de), wspec(1, de), wspec(1, de), wspec(de, d), wspec(1, d)],
        out_specs=rows)
    return pl.pallas_call(
        _expert_kernel, grid_spec=grid_spec, out_shape=jax.ShapeDtypeStruct((a, d), BF),
        compiler_params=_cparams("arbitrary"), name="moe_experts",
    )(*steps, xs, w1g, w1l, b1g, b1l, w2, b2)


def _router_kernel(x_ref, w_ref, b_ref, tri_ref, idx_ref, gate_ref, rank_ref, cnt_ref, carry_ref):
    @pl.when(pl.program_id(0) == 0)
    def _():
        carry_ref[...] = jnp.zeros_like(carry_ref)

    logits = jnp.dot(x_ref[...], w_ref[...], precision=lax.Precision.HIGHEST,
                     preferred_element_type=F32) + b_ref[...]
    lane = lax.broadcasted_iota(jnp.int32, logits.shape, 1).astype(F32)
    cur = logits
    vals, hots = [], []
    for k in range(TOP_K):
        m = jnp.max(cur, axis=-1, keepdims=True)
        idx = jnp.min(jnp.where(cur == m, lane, float(LANES)), axis=-1, keepdims=True)
        hot = lane == idx
        cur = jnp.where(hot, NEG_INF, cur)
        vals.append(m)
        hots.append(hot.astype(F32))
        idx_ref[:, k:k + 1] = idx.astype(jnp.int32)
    ex = [jnp.exp(v - vals[0]) for v in vals]
    tot = ex[0]
    for e in ex[1:]:
        tot = tot + e
    for k in range(TOP_K):
        gate_ref[:, k:k + 1] = ex[k] / tot
    chosen = hots[0]
    for h in hots[1:]:
        chosen = chosen + h
    before = jnp.dot(tri_ref[...], chosen.astype(BF), preferred_element_type=F32) + carry_ref[...]
    for k in range(TOP_K):
        rank_ref[:, k:k + 1] = jnp.sum(hots[k] * before, axis=-1, keepdims=True).astype(jnp.int32)
    carry_ref[...] = carry_ref[...] + jnp.sum(chosen, axis=0, keepdims=True)
    cnt_ref[...] = carry_ref[...].astype(jnp.int32)


def _router_call(x, router_w, router_b, tm=512):
    n, d = x.shape
    ne = router_w.shape[1]
    assert ne <= LANES
    w = jnp.pad(router_w, ((0, 0), (0, LANES - ne)))
    b = jnp.pad(router_b, (0, LANES - ne), constant_values=NEG_INF).reshape(1, LANES)
    tri = jnp.asarray(np.tril(np.ones((tm, tm), np.float32), -1)).astype(BF)
    col = lambda dt: jax.ShapeDtypeStruct((n, TOP_K), dt)
    cspec = pl.BlockSpec((tm, TOP_K), lambda i: (i, 0))
    return pl.pallas_call(
        _router_kernel, grid=(n // tm,),
        in_specs=[pl.BlockSpec((tm, d), lambda i: (i, 0)), pl.BlockSpec((d, LANES), lambda i: (0, 0)),
                  pl.BlockSpec((1, LANES), lambda i: (0, 0)), pl.BlockSpec((tm, tm), lambda i: (0, 0))],
        out_specs=[cspec, cspec, cspec, pl.BlockSpec((1, LANES), lambda i: (0, 0))],
        out_shape=[col(jnp.int32), col(F32), col(jnp.int32), jax.ShapeDtypeStruct((1, LANES), jnp.int32)],
        scratch_shapes=[pltpu.VMEM((1, LANES), F32)],
        compiler_params=_cparams("arbitrary"), name="moe_router",
    )(x, w, b, tri)


def _route(x, router_w, router_b, n_experts):
    n = x.shape[0]
    top_i, gate, rank, cnt = _router_call(x, router_w, router_b)
    a = n * TOP_K
    assert a % MOE_BLOCK == 0 and n_experts * a < 2 ** 31
    e_flat = top_i.reshape(-1)
    ids = jnp.arange(a, dtype=jnp.int32)
    skey = jnp.sort(e_flat * a + ids)
    tok_sorted = (skey % a) // TOP_K
    experts = jnp.arange(n_experts, dtype=jnp.int32)
    counts = cnt[0, :n_experts]
    ends = jnp.cumsum(counts)
    starts = ends - counts
    pos = rank + jnp.sum(jnp.where(top_i[:, :, None] == experts[None, None, :], starts[None, None, :], 0), axis=-1)
    n_tiles = a // MOE_BLOCK
    first_tile = starts // MOE_BLOCK
    ntile = jnp.where(counts > 0, (ends - 1) // MOE_BLOCK - first_tile + 1, 0)
    cum = jnp.cumsum(ntile)
    n_steps = n_tiles + n_experts - 1
    s = jnp.arange(n_steps, dtype=jnp.int32)
    g = jnp.minimum(jnp.sum((s[:, None] >= cum[None, :]).astype(jnp.int32), axis=1), n_experts - 1)
    live = s < cum[-1]
    tile = jnp.where(live, first_tile[g] + s - (cum[g] - ntile[g]), n_tiles - 1)
    lo = jnp.clip(starts[g] - tile * MOE_BLOCK, 0, MOE_BLOCK)
    hi = jnp.where(live, jnp.clip(ends[g] - tile * MOE_BLOCK, 0, MOE_BLOCK), lo)
    g = jnp.where(live, g, g[jnp.maximum(cum[-1] - 1, 0)])
    steps = tuple(v.astype(jnp.int32) for v in (g, tile, lo, hi))
    return gate, pos.reshape(n, TOP_K), tok_sorted, steps


def kernel(x, mem, ln_in_g, ln_in_b, w_in, b_gate, hy_conv_w, hy_conv_b, hy_f1_w, hy_f1_b, hy_f2_w, hy_f2_b, hy_f3_w, hy_freq, hy_skip, sw_sink, mem_w_kv, w_br_hy, w_br_swa, w_br_mem, w_out, ln1_g, ln1_b, router_w, router_b, moe_w1, moe_b1, moe_w2, moe_b2, ln2_g, ln2_b):
    batch, seq, d = x.shape
    depth = w_in.shape[0]
    n = batch * seq
    width = hy_skip.shape[2]
    hy_cols = 3 * width
    q_cols = SW_HEADS * SW_HEAD_DIM
    kv_cols = SW_KV_HEADS * SW_HEAD_DIM
    mq_cols = MEM_HEADS * MEM_HEAD_DIM
    gate_cols = N_BRANCH * d
    mem_tokens = mem.shape[1]
    n_experts = router_w.shape[2]
    alpha = (2 * depth) ** 0.25
    assert batch == 2, "the long convolution packs exactly two batch elements into one complex signal"
    assert w_in.shape[2] == hy_cols + q_cols + 2 * kv_cols + mq_cols + gate_cols
    n1 = 2 * seq // DFT_N2

    cst = _dft_consts(n1, DFT_N2)
    zfeat = _filter_features(seq)
    cos_t, sin_t = _rope_tables(seq)
    swap = _rope_swap_matrix()

    o_q = hy_cols
    o_k = o_q + q_cols
    o_v = o_k + kv_cols
    o_mq = o_v + kv_cols
    o_g = o_mq + mq_cols
    dup = lambda w0: jnp.concatenate(
        [w_in[:, :, w0 + h * SW_HEAD_DIM:w0 + (h + 1) * SW_HEAD_DIM] for h in range(SW_KV_HEADS) for _ in range(2)],
        axis=2)
    w_proj = jnp.concatenate([w_in[:, :, o_g:o_g + gate_cols], w_in[:, :, :hy_cols], w_in[:, :, o_q:o_q + q_cols],
                              w_in[:, :, o_mq:o_mq + mq_cols], dup(o_k), dup(o_v)], axis=2).astype(BF)
    kw = 2 * kv_cols
    c_hy, c_q, c_mq = gate_cols, gate_cols + hy_cols, gate_cols + hy_cols + q_cols
    c_k = c_mq + mq_cols
    c_v = c_k + kw
    proj_cols = c_v + kw
    tn_proj = proj_cols // 4

    w1g, w1l = _w1_prep_call(moe_w1)
    de = w1g.shape[-1]
    b1g = moe_b1[..., 0::2].reshape(depth, n_experts, 1, de)
    b1l = moe_b1[..., 1::2].reshape(depth, n_experts, 1, de)
    w2 = _w2_cast_call(moe_w2)
    b2 = moe_b2.reshape(depth, n_experts, 1, d)
    mem_bf = mem.reshape(batch * mem_tokens, d).astype(BF)

    xf, xb = _ln_call(x.reshape(n, d), ln_in_g, ln_in_b)
    for l in range(depth):
        proj = _mm_call(xb, w_proj[l], BF, 1024, tn_proj, "in_proj")

        hv, hx1, hx2 = _shortconv_call(proj, hy_conv_w[l], hy_conv_b[l], batch, seq, c_hy // hy_cols)
        af = _hyena_filter_stage1(seq, width, hy_f1_w[l], hy_f1_b[l], hy_f2_w[l], hy_f2_b[l],
                                  hy_f3_w[l], hy_freq[l], zfeat, cst)
        y_hy = hv
        for o, gx in enumerate((hx1, hx2)):
            y_hy = _hyena_conv(y_hy, gx, af[o], hy_skip[l, o], cst, seq, width)

        q_rot, k_rot = _rope_call(proj, cos_t, sin_t, swap, batch, seq, c_q // q_cols, c_k // kw, q_cols, kw)
        y_sw = _attn_call(q_rot, k_rot, proj[:, c_v:c_v + kw].T, sw_sink[l], batch, seq)

        memkv = _mm_call(mem_bf, mem_w_kv[l].astype(BF), BF, mem_tokens, 2 * mq_cols, "mem_kv")
        y_mem = _memattn_call(proj, c_mq // mq_cols, memkv, batch, seq, mem_tokens)

        xf, xb = _merge_call(xf, proj, 0, b_gate[l], y_hy, y_sw, y_mem, w_br_hy[l].astype(BF),
                             w_br_swa[l].astype(BF), w_br_mem[l].astype(BF), w_out[l].astype(BF),
                             ln1_g[l], ln1_b[l], alpha)

        gate, dest, tok_sorted, steps = _route(xf, router_w[l], router_b[l], n_experts)
        xs = xb.at[tok_sorted].get(mode="promise_in_bounds")
        ys = _expert_call(steps, xs, w1g, w1l, b1g, b1l, w2, b2, l)
        yg = ys.at[dest.T].get(mode="promise_in_bounds")
        xf, xb = _combine_ln_call(xf, yg, gate, ln2_g[l], ln2_b[l], alpha)
    return xf.reshape(batch, seq, d)
```

```python
import functools
import math

import numpy as np
import jax
import jax.numpy as jnp
from jax import lax
from jax.experimental import pallas as pl
from jax.experimental.pallas import tpu as pltpu

BF = jnp.bfloat16
F32 = jnp.float32

HY_ORDER = 2
HY_SHORT = 3
HY_EMB = 33
HY_BANDS = (HY_EMB - 1) // 2
HY_FAST_DECAY = 0.3
HY_SLOW_DECAY = 1.5
HY_DECAY_TARGET = 1e-2
SW_HEADS = 8
SW_KV_HEADS = 2
SW_HEAD_DIM = 64
SW_WINDOW = 128
ROPE_THETA = 500000.0
ROPE_DIM = SW_HEAD_DIM // 4
MEM_HEADS = 4
MEM_HEAD_DIM = 128
N_BRANCH = 3
TOP_K = 4
SWIGLU_ALPHA = 1.702
SWIGLU_LIMIT = 7.0
MOE_BLOCK = 512
LN_EPS = 1e-5
NEG_INF = -1e30

LANES = 128
V7X_VMEM_BYTES = 64 * 1024 * 1024
VMEM_LIMIT = 48 * 1024 * 1024
VMEM_LIMIT_EXPERTS = 58 * 1024 * 1024
DFT_N2 = 128
SUBLANES = 8
FEAT_ROWS = -(-HY_EMB // SUBLANES) * SUBLANES


def _cparams(*sem):
    return pltpu.CompilerParams(dimension_semantics=sem, vmem_limit_bytes=VMEM_LIMIT)


def _ln_rows(r, g, b):
    mu = jnp.mean(r, axis=-1, keepdims=True)
    d = r - mu
    var = jnp.mean(d * d, axis=-1, keepdims=True)
    return d * lax.rsqrt(var + LN_EPS) * g + b


def _ln_kernel(x_ref, g_ref, b_ref, y_ref, yb_ref):
    y = _ln_rows(x_ref[...], g_ref[...], b_ref[...])
    y_ref[...] = y
    yb_ref[...] = y.astype(BF)


def _ln_call(x, g, b, tm=512):
    n, d = x.shape
    row = pl.BlockSpec((tm, d), lambda i: (i, 0))
    vec = pl.BlockSpec((1, d), lambda i: (0, 0))
    return pl.pallas_call(
        _ln_kernel, grid=(n // tm,), in_specs=[row, vec, vec], out_specs=[row, row],
        out_shape=[jax.ShapeDtypeStruct((n, d), F32), jax.ShapeDtypeStruct((n, d), BF)],
        compiler_params=_cparams("parallel"), name="ln_entry",
    )(x, g.reshape(1, d), b.reshape(1, d))


def _combine_ln_kernel(x_ref, y_ref, w_ref, g_ref, b_ref, o_ref, ob_ref, *, alpha):
    w = w_ref[...]
    f = y_ref[0].astype(F32) * w[:, 0:1]
    for k in range(1, TOP_K):
        f = f + y_ref[k].astype(F32) * w[:, k:k + 1]
    y = _ln_rows(alpha * x_ref[...] + f, g_ref[...], b_ref[...])
    o_ref[...] = y
    ob_ref[...] = y.astype(BF)


def _combine_ln_call(x, yg, w, g, b, alpha, tm=512):
    n, d = x.shape
    row = pl.BlockSpec((tm, d), lambda i: (i, 0))
    vec = pl.BlockSpec((1, d), lambda i: (0, 0))
    return pl.pallas_call(
        functools.partial(_combine_ln_kernel, alpha=alpha), grid=(n // tm,),
        in_specs=[row, pl.BlockSpec((TOP_K, tm, d), lambda i: (0, i, 0)),
                  pl.BlockSpec((tm, TOP_K), lambda i: (i, 0)), vec, vec],
        out_specs=[row, row],
        out_shape=[jax.ShapeDtypeStruct((n, d), F32), jax.ShapeDtypeStruct((n, d), BF)],
        compiler_params=_cparams("parallel"), name="moe_combine_ln",
    )(x, yg, w, g.reshape(1, d), b.reshape(1, d))


def _mm_kernel(a_ref, b_ref, o_ref):
    o_ref[...] = jnp.dot(a_ref[...], b_ref[...], preferred_element_type=F32).astype(o_ref.dtype)


def _mm_call(a, b, out_dtype, tm, tn, name):
    m, k = a.shape
    n = b.shape[1]
    return pl.pallas_call(
        _mm_kernel, grid=(n // tn, m // tm),
        in_specs=[pl.BlockSpec((tm, k), lambda j, i: (i, 0)), pl.BlockSpec((k, tn), lambda j, i: (0, j))],
        out_specs=pl.BlockSpec((tm, tn), lambda j, i: (i, j)),
        out_shape=jax.ShapeDtypeStruct((m, n), out_dtype),
        compiler_params=_cparams("parallel", "parallel"), name=name,
    )(a, b)


def _shortconv_kernel(zp_ref, z_ref, zn_ref, w_ref, b_ref, hv_ref, hx1_ref, hx2_ref, *, tl, halo, width):
    i = pl.program_id(1)
    nt = pl.num_programs(1)
    z = z_ref[...].astype(F32)
    prev_row = zp_ref[halo - 1:halo, :].astype(F32)
    next_row = zn_ref[0:1, :].astype(F32)
    prev_row = jnp.where(i == 0, 0.0, prev_row)
    next_row = jnp.where(i == nt - 1, 0.0, next_row)
    row = lax.broadcasted_iota(jnp.int32, z.shape, 0)
    zm1 = jnp.where(row == 0, prev_row, pltpu.roll(z, 1, 0))
    zp1 = jnp.where(row == tl - 1, next_row, pltpu.roll(z, tl - 1, 0))
    w = w_ref[...]
    out = zm1 * w[0:1] + b_ref[...] + z * w[1:2] + zp1 * w[2:3]
    hv_ref[...] = out[:, 0:width].astype(BF)
    hx1_ref[...] = out[:, width:2 * width].astype(BF)
    hx2_ref[...] = out[:, 2 * width:3 * width].astype(BF)


def _shortconv_call(proj, conv_w, conv_b, batch, seq, col_blk, tl=512, halo=16):
    n = proj.shape[0]
    c3 = conv_w.shape[1]
    width = c3 // 3
    nt = seq // tl
    hb = tl // halo
    nhb = seq // halo
    main = pl.BlockSpec((tl, c3), lambda b, i: (b * nt + i, col_blk))
    prev = pl.BlockSpec((halo, c3), lambda b, i: (b * nhb + jnp.maximum(i * hb - 1, 0), col_blk))
    nxt = pl.BlockSpec((halo, c3), lambda b, i: (b * nhb + jnp.minimum((i + 1) * hb, nhb - 1), col_blk))
    out = pl.BlockSpec((tl, width), lambda b, i: (b * nt + i, 0))
    return pl.pallas_call(
        functools.partial(_shortconv_kernel, tl=tl, halo=halo, width=width), grid=(batch, nt),
        in_specs=[prev, main, nxt, pl.BlockSpec((HY_SHORT, c3), lambda b, i: (0, 0)),
                  pl.BlockSpec((1, c3), lambda b, i: (0, 0))],
        out_specs=[out, out, out],
        out_shape=[jax.ShapeDtypeStruct((n, width), BF)] * 3,
        compiler_params=_cparams("parallel", "parallel"), name="hy_shortconv",
    )(proj, proj, proj, conv_w, conv_b.reshape(1, c3))


def _filter_kernel(z_ref, f1w_ref, f1b_ref, f2w_ref, f2b_ref, f3w_ref, freq_ref, adel_ref, k_ref,
                   *, tr, seq, n_inner):
    i = pl.program_id(0)
    hp = lax.Precision.HIGHEST
    fr = freq_ref[...]
    h = jnp.sin(fr * (jnp.dot(f1w_ref[...], z_ref[...], precision=hp, preferred_element_type=F32)
                      + f1b_ref[...]))
    for j in range(n_inner):
        h = jnp.sin(fr * (jnp.dot(f2w_ref[j], h, precision=hp, preferred_element_type=F32) + f2b_ref[j]))
    k = lax.dot_general(h.astype(BF), f3w_ref[...], (((0,), (0,)), ((), ())),
                        preferred_element_type=F32)
    width = adel_ref.shape[1]
    row = i * tr + lax.broadcasted_iota(jnp.int32, (tr, width), 0)
    pos = jnp.where(row < seq, row, 2 * seq - row)
    win = jnp.exp(-(pos.astype(F32) * (1.0 / (seq - 1))) * adel_ref[...])
    win = jnp.where(row == seq, 0.0, win)
    for o in range(HY_ORDER):
        k_ref[o] = (k[:, o * width:(o + 1) * width] * win).astype(k_ref.dtype)


def _filter_call(zfeat_t, f1w_t, f1b, f2w_t, f2b, f3w_dir, freq, adel, seq, tr=1024):
    zr, n2l = zfeat_t.shape
    hid = f1w_t.shape[0]
    n_inner = f2w_t.shape[0]
    ow = f3w_dir.shape[2]
    width = adel.shape[1]
    half = (n2l // tr) // 2
    full = lambda *shape: pl.BlockSpec(shape, lambda i: (0,) * len(shape))
    return pl.pallas_call(
        functools.partial(_filter_kernel, tr=tr, seq=seq, n_inner=n_inner), grid=(n2l // tr,),
        in_specs=[pl.BlockSpec((zr, tr), lambda i: (0, i)), full(hid, zr), full(hid, 1),
                  full(n_inner, hid, hid), full(n_inner, hid, 1),
                  pl.BlockSpec((None, hid, ow), lambda i: (i // half, 0, 0)),
                  full(hid, 1), full(1, width)],
        out_specs=pl.BlockSpec((HY_ORDER, tr, width), lambda i: (0, i, 0)),
        out_shape=jax.ShapeDtypeStruct((HY_ORDER, n2l, width), BF),
        compiler_params=_cparams("parallel"), name="hy_filter",
    )(zfeat_t, f1w_t, f1b, f2w_t, f2b, f3w_dir, freq, adel)


def _dft_consts(n1, n2):
    n = n1 * n2
    h1 = n1 // 2
    k1 = np.arange(n1)[:, None]
    ang = -2.0 * np.pi * (k1 * np.arange(h1)[None, :]) / n1
    cr, ci = np.cos(ang), np.sin(ang)
    e1 = np.block([[cr, -ci], [ci, cr]])
    angf = -2.0 * np.pi * (k1 * np.arange(n1)[None, :]) / n1
    ef = np.concatenate([np.cos(angf), np.sin(angf)], axis=0)
    a2 = -2.0 * np.pi * (np.arange(n2)[:, None] * np.arange(n2)[None, :]) / n2
    f2r, f2i = np.cos(a2), np.sin(a2)
    dm = np.block([[f2r, f2i], [-f2i, f2r]])
    at = -2.0 * np.pi * (np.arange(n1)[:, None] * np.arange(n2)[None, :]) / n
    twr, twi = np.cos(at), np.sin(at)
    ab = 2.0 * np.pi * (np.arange(h1)[:, None] * np.arange(n1)[None, :]) / n1
    br, bi = np.cos(ab) / n, np.sin(ab) / n
    f = lambda a: jnp.asarray(a, dtype=F32)
    return dict(e1=f(e1), ef=f(ef), f2r=f(f2r), f2i=f(f2i), dm=f(dm),
                twr=f(twr.reshape(n1, 1, n2)), twi=f(twi.reshape(n1, 1, n2)),
                ctr=f(twr.T.reshape(n2, 1, n1)), cti=f(-twi.T.reshape(n2, 1, n1)),
                br=f(br), bi=f(bi))


def _dft_s1_kernel(e_ref, top_ref, bot_ref, a_ref):
    rhs = jnp.concatenate([top_ref[...], bot_ref[...]], axis=0).astype(BF)
    res = jnp.dot(e_ref[...], rhs, preferred_element_type=F32)
    n1 = a_ref.shape[1]
    a_ref[0] = res[:n1].astype(a_ref.dtype)
    a_ref[1] = res[n1:].astype(a_ref.dtype)


def _dft_s1_call(e_mat, x3, top_idx, bot_idx, n1, tn=2048):
    h1, cols = x3.shape[1], x3.shape[2]
    return pl.pallas_call(
        _dft_s1_kernel, grid=(cols // tn,),
        in_specs=[pl.BlockSpec((2 * n1, 2 * h1), lambda j: (0, 0)),
                  pl.BlockSpec((None, h1, tn), lambda j: (top_idx, 0, j)),
                  pl.BlockSpec((None, h1, tn), lambda j: (bot_idx, 0, j))],
        out_specs=pl.BlockSpec((2, n1, tn), lambda j: (0, 0, j)),
        out_shape=jax.ShapeDtypeStruct((2, n1, cols), BF),
        compiler_params=_cparams("parallel"), name="hy_dft_stage1",
    )(e_mat.astype(BF), x3, x3)


def _twiddled_f2(f2r, f2i, tr, ti):
    gr = f2r * tr - f2i * ti
    gi = f2r * ti + f2i * tr
    return jnp.concatenate([jnp.concatenate([gr, -gi], axis=1), jnp.concatenate([gi, gr], axis=1)], axis=0)


def _dft_s2_conv_kernel(a_ref, af_ref, f2r_ref, f2i_ref, dm_ref, twr_ref, twi_ref, b_ref, *, kb):
    n2 = f2r_ref.shape[0]
    dm = dm_ref[...]
    for j in range(kb):
        gm = _twiddled_f2(f2r_ref[...], f2i_ref[...], twr_ref[j], twi_ref[j]).astype(BF)
        rhs = jnp.concatenate([a_ref[0, j], a_ref[1, j]], axis=0)
        x = jnp.dot(gm, rhs, preferred_element_type=F32)
        kf = jnp.dot(gm, jnp.concatenate([af_ref[0, j], af_ref[1, j]], axis=0),
                     preferred_element_type=F32)
        xr, xi = x[:n2], x[n2:]
        kr, ki = kf[:n2], kf[n2:]
        y = jnp.concatenate([xr * kr - xi * ki, xr * ki + xi * kr], axis=0).astype(BF)
        bm = jnp.dot(dm, y, preferred_element_type=F32)
        b_ref[0, j] = bm[:n2].astype(b_ref.dtype)
        b_ref[1, j] = bm[n2:].astype(b_ref.dtype)


def _dft_s2_conv_call(a4, af4, cst, kb=8):
    _, n1, n2, c = a4.shape
    blk = pl.BlockSpec((2, kb, n2, c), lambda i: (0, i, 0, 0))
    sq = pl.BlockSpec((n2, n2), lambda i: (0, 0))
    tw = pl.BlockSpec((kb, 1, n2), lambda i: (i, 0, 0))
    return pl.pallas_call(
        functools.partial(_dft_s2_conv_kernel, kb=kb), grid=(n1 // kb,),
        in_specs=[blk, blk, sq, sq, pl.BlockSpec((2 * n2, 2 * n2), lambda i: (0, 0)), tw, tw],
        out_specs=blk, out_shape=jax.ShapeDtypeStruct(a4.shape, BF),
        compiler_params=_cparams("parallel"), name="hy_dft_stage2_conv",
    )(a4, af4, cst["f2r"], cst["f2i"], cst["dm"].astype(BF), cst["twr"], cst["twi"])


def _dft_s1inv_kernel(b_ref, br_ref, bi_ref, ctr_ref, cti_ref, u_ref, gx_ref, skip_ref, o_ref, *, ng, c):
    h1 = br_ref.shape[0]
    skip = skip_ref[...]
    for j in range(ng):
        tr, ti = ctr_ref[j], cti_ref[j]
        mr = br_ref[...] * tr - bi_ref[...] * ti
        mi = br_ref[...] * ti + bi_ref[...] * tr
        em = jnp.concatenate([jnp.concatenate([mr, -mi], axis=1),
                              jnp.concatenate([mi, mr], axis=1)], axis=0).astype(BF)
        y = jnp.dot(em, b_ref[:, j * c:(j + 1) * c], preferred_element_type=F32)
        for b in range(2):
            u = u_ref[b, :, j * c:(j + 1) * c].astype(F32)
            g = gx_ref[b, :, j * c:(j + 1) * c].astype(F32)
            o_ref[b, :, j * c:(j + 1) * c] = (g * (y[b * h1:(b + 1) * h1] + skip * u)).astype(o_ref.dtype)


def _dft_s1inv_call(b2, u3, gx3, skip, cst, c, ng=4):
    rows, cols = b2.shape
    h1 = u3.shape[1]
    n2 = cols // c
    n1 = rows // 2
    ub = pl.BlockSpec((2, h1, ng * c), lambda i: (0, 0, i))
    cb = pl.BlockSpec((h1, n1), lambda i: (0, 0))
    tw = pl.BlockSpec((ng, 1, n1), lambda i: (i, 0, 0))
    return pl.pallas_call(
        functools.partial(_dft_s1inv_kernel, ng=ng, c=c), grid=(n2 // ng,),
        in_specs=[pl.BlockSpec((rows, ng * c), lambda i: (0, i)), cb, cb, tw, tw, ub, ub,
                  pl.BlockSpec((1, c), lambda i: (0, 0))],
        out_specs=ub, out_shape=jax.ShapeDtypeStruct(u3.shape, BF),
        compiler_params=_cparams("parallel"), name="hy_dft_stage1_inverse",
    )(b2, cst["br"], cst["bi"], cst["ctr"], cst["cti"], u3, gx3, skip.reshape(1, c))


def _filter_features(seq):
    t01 = jnp.linspace(0.0, 1.0, seq, dtype=F32)[:, None]
    w = (2.0 * math.pi) * jnp.arange(seq, dtype=F32)[:, None] / seq
    bands = jnp.linspace(1e-4, HY_BANDS - 1, HY_BANDS, dtype=F32)
    z = jnp.concatenate([t01, jnp.cos(bands * w), -jnp.sin(bands * w)], axis=-1)
    z2 = jnp.concatenate([z, z[:1], z[1:][::-1]], axis=0)
    return jnp.pad(z2, ((0, 0), (0, FEAT_ROWS - HY_EMB))).T


def _hyena_filter_stage1(seq, width, f1_w, f1_b, f2_w, f2_b, f3_w, freq, zfeat_t, cst):
    hid = f1_w.shape[1]
    n1 = 2 * seq // DFT_N2
    max_decay = math.log(HY_DECAY_TARGET) / HY_FAST_DECAY
    min_decay = math.log(HY_DECAY_TARGET) / HY_SLOW_DECAY
    adel = jnp.abs(jnp.linspace(min_decay, max_decay, width, dtype=F32)).reshape(1, width)
    f1w_t = jnp.pad(f1_w, ((0, FEAT_ROWS - HY_EMB), (0, 0))).T
    f3d = f3_w.reshape(hid, HY_ORDER, 2, width).transpose(2, 0, 1, 3).reshape(2, hid, HY_ORDER * width)
    k3 = _filter_call(zfeat_t, f1w_t, f1_b.reshape(hid, 1), f2_w.transpose(0, 2, 1), f2_b.reshape(-1, hid, 1),
                      f3d.astype(BF), freq.reshape(hid, 1), adel, seq)
    kw = k3.reshape(HY_ORDER * 2, n1 // 2, DFT_N2 * width)
    return [_dft_s1_call(cst["ef"], kw, 2 * o, 2 * o + 1, n1).reshape(2, n1, DFT_N2, width)
            for o in range(HY_ORDER)]


def _hyena_conv(u3, gx3, af, skip, cst, seq, width):
    n1 = 2 * seq // DFT_N2
    a = _dft_s1_call(cst["e1"], u3, 0, 1, n1)
    bm = _dft_s2_conv_call(a.reshape(2, n1, DFT_N2, width), af, cst)
    return _dft_s1inv_call(bm.reshape(2 * n1, DFT_N2 * width), u3, gx3, skip, cst, width)


def _rope_tables(seq):
    half = ROPE_DIM // 2
    inv = jnp.power(jnp.float32(ROPE_THETA), -jnp.arange(half, dtype=F32) * (2.0 / ROPE_DIM))
    ang = jnp.arange(seq, dtype=jnp.int32).astype(F32)[:, None] * inv
    cos, sin = jnp.cos(ang), jnp.sin(ang)
    rest = SW_HEAD_DIM - ROPE_DIM
    ones = jnp.ones((seq, rest), F32)
    zeros = jnp.zeros((seq, rest), F32)
    ch = jnp.concatenate([cos, cos, ones], axis=1)
    sh = jnp.concatenate([-sin, sin, zeros], axis=1)
    reps = LANES // SW_HEAD_DIM
    return jnp.tile(ch, (1, reps)), jnp.tile(sh, (1, reps))


def _rope_swap_matrix():
    half = ROPE_DIM // 2
    p = np.zeros((LANES, LANES), np.float32)
    for j in range(LANES):
        d = j % SW_HEAD_DIM
        if d < half:
            p[j + half, j] = 1.0
        elif d < ROPE_DIM:
            p[j - half, j] = 1.0
    return jnp.asarray(p)


def _rope_kernel(q_ref, k_ref, c_ref, s_ref, p_ref, qo_ref, ko_ref, *, scale):
    c, s, p = c_ref[...], s_ref[...], p_ref[...]

    def rot(x):
        sw = jnp.dot(x, p, preferred_element_type=F32)
        return x.astype(F32) * c + sw * s

    for g in range(q_ref.shape[1] // LANES):
        qo_ref[:, g * LANES:(g + 1) * LANES] = (rot(q_ref[:, g * LANES:(g + 1) * LANES]) * scale).astype(BF)
    for g in range(k_ref.shape[1] // LANES):
        ko_ref[:, g * LANES:(g + 1) * LANES] = rot(k_ref[:, g * LANES:(g + 1) * LANES]).astype(BF)


def _rope_call(proj, cos_t, sin_t, swap, batch, seq, q_blk, k_blk, qw, kw, tm=512):
    n = proj.shape[0]
    nt = seq // tm
    tab = pl.BlockSpec((tm, LANES), lambda b, i: (i, 0))
    return pl.pallas_call(
        functools.partial(_rope_kernel, scale=SW_HEAD_DIM ** -0.5), grid=(batch, nt),
        in_specs=[pl.BlockSpec((tm, qw), lambda b, i: (b * nt + i, q_blk)),
                  pl.BlockSpec((tm, kw), lambda b, i: (b * nt + i, k_blk)),
                  tab, tab, pl.BlockSpec((LANES, LANES), lambda b, i: (0, 0))],
        out_specs=[pl.BlockSpec((tm, qw), lambda b, i: (b * nt + i, 0)),
                   pl.BlockSpec((tm, kw), lambda b, i: (b * nt + i, 0))],
        out_shape=[jax.ShapeDtypeStruct((n, qw), BF), jax.ShapeDtypeStruct((n, kw), BF)],
        compiler_params=_cparams("parallel", "parallel"), name="sw_rope",
    )(proj, proj, cos_t, sin_t, swap.astype(BF))


def _attn_kernel(sink_ref, q_ref, kp_ref, km_ref, kn_ref, vp_ref, vm_ref, vn_ref, o_ref, *, tq, seq):
    blk = SW_WINDOW
    i = pl.program_id(1)
    kwin = jnp.concatenate([kp_ref[...], km_ref[...], kn_ref[...]], axis=0)
    vwin_t = jnp.concatenate([vp_ref[...], vm_ref[...], vn_ref[...]], axis=1)
    lo = lax.broadcasted_iota(jnp.int32, (blk, LANES), 1) < SW_HEAD_DIM
    zero = jnp.zeros((), BF)
    per_group = LANES // SW_HEAD_DIM
    groups_per_kv = (SW_HEADS // SW_KV_HEADS) // per_group
    stack = groups_per_kv * per_group
    w_idx = lax.broadcasted_iota(jnp.int32, (3 * blk, stack * blk), 0)
    a_idx = lax.broadcasted_iota(jnp.int32, (3 * blk, stack * blk), 1) & (blk - 1)
    rel = w_idx - a_idx
    for jb in range(tq // blk):
        kpos = i * tq + (jb - 1) * blk + w_idx
        bad = (rel < 0) | (rel > 2 * SW_WINDOW) | (kpos < 0) | (kpos >= seq)
        for h in range(SW_KV_HEADS):
            parts, sinks = [], []
            for gg in range(groups_per_kv):
                g = h * groups_per_kv + gg
                qg = q_ref[jb * blk:(jb + 1) * blk, g * LANES:(g + 1) * LANES]
                parts += [jnp.where(lo, qg, zero), jnp.where(lo, zero, qg)]
                sinks += [jnp.full((1, blk), sink_ref[g * per_group + par], F32) for par in range(per_group)]
            qs = jnp.concatenate(parts, axis=0)
            sk = jnp.concatenate(sinks, axis=1)
            kh = kwin[jb * blk:(jb + 3) * blk, h * LANES:(h + 1) * LANES]
            vt = vwin_t[h * LANES:(h + 1) * LANES, jb * blk:(jb + 3) * blk]
            s = lax.dot_general(kh, qs, (((1,), (1,)), ((), ())), preferred_element_type=F32)
            s = jnp.where(bad, NEG_INF, s)
            m = jnp.maximum(jnp.max(s, axis=0, keepdims=True), sk)
            p = jnp.exp(s - m)
            denom = jnp.sum(p, axis=0, keepdims=True) + jnp.exp(sk - m)
            ot = jnp.dot(vt, p.astype(BF), preferred_element_type=F32) / denom
            o = ot.T
            for gg in range(groups_per_kv):
                g = h * groups_per_kv + gg
                oa = o[(gg * per_group) * blk:(gg * per_group + 1) * blk]
                ob = o[(gg * per_group + 1) * blk:(gg * per_group + 2) * blk]
                o_ref[jb * blk:(jb + 1) * blk, g * LANES:(g + 1) * LANES] = jnp.where(lo, oa, ob).astype(BF)


def _attn_call(q_rot, k_rot, v_t, sink, batch, seq, tq=256):
    n, qw = q_rot.shape
    kw = k_rot.shape[1]
    blk = SW_WINDOW
    nt = seq // tq
    r = tq // blk
    nb = seq // blk
    prev_i = lambda b, i: b * nb + jnp.maximum(i * r - 1, 0)
    next_i = lambda b, i: b * nb + jnp.minimum((i + 1) * r, nb - 1)
    qspec = pl.BlockSpec((tq, qw), lambda b, i, s: (b * nt + i, 0))
    grid_spec = pltpu.PrefetchScalarGridSpec(
        num_scalar_prefetch=1, grid=(batch, nt),
        in_specs=[qspec,
                  pl.BlockSpec((blk, kw), lambda b, i, s: (prev_i(b, i), 0)),
                  pl.BlockSpec((tq, kw), lambda b, i, s: (b * nt + i, 0)),
                  pl.BlockSpec((blk, kw), lambda b, i, s: (next_i(b, i), 0)),
                  pl.BlockSpec((kw, blk), lambda b, i, s: (0, prev_i(b, i))),
                  pl.BlockSpec((kw, tq), lambda b, i, s: (0, b * nt + i)),
                  pl.BlockSpec((kw, blk), lambda b, i, s: (0, next_i(b, i)))],
        out_specs=qspec)
    return pl.pallas_call(
        functools.partial(_attn_kernel, tq=tq, seq=seq), grid_spec=grid_spec,
        out_shape=jax.ShapeDtypeStruct((n, qw), BF),
        compiler_params=_cparams("parallel", "parallel"), name="sw_attention",
    )(sink, q_rot, k_rot, k_rot, k_rot, v_t, v_t, v_t)


def _memattn_kernel(q_ref, mk_ref, mv_ref, o_ref):
    scale = MEM_HEAD_DIM ** -0.5
    for h in range(MEM_HEADS):
        sl = slice(h * MEM_HEAD_DIM, (h + 1) * MEM_HEAD_DIM)
        s = lax.dot_general(q_ref[:, sl], mk_ref[:, sl], (((1,), (1,)), ((), ())),
                            preferred_element_type=F32) * scale
        m = jnp.max(s, axis=-1, keepdims=True)
        p = jnp.exp(s - m)
        denom = jnp.sum(p, axis=-1, keepdims=True)
        o = jnp.dot(p.astype(BF), mv_ref[:, sl], preferred_element_type=F32) / denom
        o_ref[:, sl] = o.astype(BF)


def _memattn_call(proj, q_blk, memkv, batch, seq, mem_tokens, tm=512):
    n = proj.shape[0]
    w = MEM_HEADS * MEM_HEAD_DIM
    nt = seq // tm
    return pl.pallas_call(
        _memattn_kernel, grid=(batch, nt),
        in_specs=[pl.BlockSpec((tm, w), lambda b, i: (b * nt + i, q_blk)),
                  pl.BlockSpec((mem_tokens, w), lambda b, i: (b, 0)),
                  pl.BlockSpec((mem_tokens, w), lambda b, i: (b, 1))],
        out_specs=pl.BlockSpec((tm, w), lambda b, i: (b * nt + i, 0)),
        out_shape=jax.ShapeDtypeStruct((n, w), BF),
        compiler_params=_cparams("parallel", "parallel"), name="mem_attention",
    )(proj, memkv, memkv)


def _merge_kernel(x_ref, gl_ref, bg_ref, yh_ref, ys_ref, ym_ref, wh_ref, ws_ref, wm_ref, wo_ref,
                  g_ref, b_ref, o_ref, ob_ref, *, alpha, d):
    merged = None
    for br, (y_ref, w_ref) in enumerate(((yh_ref, wh_ref), (ys_ref, ws_ref), (ym_ref, wm_ref))):
        gate = jax.nn.sigmoid(gl_ref[:, br * d:(br + 1) * d].astype(F32) + bg_ref[br:br + 1, :])
        t = gate * jnp.dot(y_ref[...], w_ref[...], preferred_element_type=F32)
        merged = t if merged is None else merged + t
    h = jnp.dot(merged.astype(BF), wo_ref[...], preferred_element_type=F32)
    y = _ln_rows(alpha * x_ref[...] + h, g_ref[...], b_ref[...])
    o_ref[...] = y
    ob_ref[...] = y.astype(BF)


def _merge_call(x, proj, gate_blk, b_gate, y_hy, y_sw, y_mem, w_hy, w_sw, w_mem, w_out, g, b, alpha, tm=512):
    n, d = x.shape
    row = lambda w: pl.BlockSpec((tm, w), lambda i: (i, 0))
    full = lambda a: pl.BlockSpec(a.shape, lambda i: (0,) * a.ndim)
    g2, b2 = g.reshape(1, d), b.reshape(1, d)
    return pl.pallas_call(
        functools.partial(_merge_kernel, alpha=alpha, d=d), grid=(n // tm,),
        in_specs=[row(d), pl.BlockSpec((tm, N_BRANCH * d), lambda i: (i, gate_blk)), full(b_gate),
                  row(y_hy.shape[1]), row(y_sw.shape[1]), row(y_mem.shape[1]),
                  full(w_hy), full(w_sw), full(w_mem), full(w_out), full(g2), full(b2)],
        out_specs=[row(d), row(d)],
        out_shape=[jax.ShapeDtypeStruct((n, d), F32), jax.ShapeDtypeStruct((n, d), BF)],
        compiler_params=_cparams("parallel"), name="merge_outproj_ln",
    )(x, proj, b_gate, y_hy, y_sw, y_mem, w_hy, w_sw, w_mem, w_out, g2, b2)


def _deinterleave_matrix():
    perm = np.zeros((2 * LANES, 2 * LANES), np.float32)
    perm[2 * np.arange(LANES), np.arange(LANES)] = 1.0
    perm[2 * np.arange(LANES) + 1, LANES + np.arange(LANES)] = 1.0
    return jnp.asarray(perm).astype(BF)


def _expert_kernel(sg_ref, st_ref, lo_ref, hi_ref, x_ref, w1_ref, b1g_ref, b1l_ref, w2_ref, b2_ref, perm_ref,
                   y_ref, w1g_ref, w1l_ref, w2b_ref):
    s = pl.program_id(0)
    lo, hi = lo_ref[s], hi_ref[s]
    prev = jnp.maximum(s - 1, 0)

    @pl.when((s == 0) | (sg_ref[s] != sg_ref[prev]))
    def _():
        p = perm_ref[...]
        for c in range(w1g_ref.shape[1] // LANES):
            r = jnp.dot(w1_ref[:, 2 * c * LANES:2 * (c + 1) * LANES].astype(BF), p, preferred_element_type=F32)
            w1g_ref[:, c * LANES:(c + 1) * LANES] = r[:, :LANES].astype(BF)
            w1l_ref[:, c * LANES:(c + 1) * LANES] = r[:, LANES:].astype(BF)
        w2b_ref[...] = w2_ref[...].astype(BF)

    @pl.when(hi > lo)
    def _():
        x = x_ref[...]
        hg = jnp.dot(x, w1g_ref[...], preferred_element_type=F32) + b1g_ref[...]
        hl = jnp.dot(x, w1l_ref[...], preferred_element_type=F32) + b1l_ref[...]
        hg = jnp.minimum(hg, SWIGLU_LIMIT)
        hl = jnp.clip(hl, -SWIGLU_LIMIT, SWIGLU_LIMIT)
        act = hg * jax.nn.sigmoid(SWIGLU_ALPHA * hg) * (hl + 1.0)
        y = jnp.dot(act.astype(BF), w2b_ref[...], preferred_element_type=F32) + b2_ref[...]
        whole = (lo == 0) & (hi == y.shape[0])
        first_visit = (s == 0) | (st_ref[s] != st_ref[prev])

        @pl.when(whole)
        def _():
            y_ref[...] = y.astype(y_ref.dtype)

        def masked(prev):
            row = lax.broadcasted_iota(jnp.int32, y.shape, 0)
            y_ref[...] = jnp.where((row >= lo) & (row < hi), y, prev).astype(y_ref.dtype)

        @pl.when(jnp.logical_not(whole) & first_visit)
        def _():
            masked(jnp.zeros_like(y))

        @pl.when(jnp.logical_not(whole) & jnp.logical_not(first_visit))
        def _():
            masked(y_ref[...].astype(F32))


def _expert_call(steps, xs, w1, b1g, b1l, w2, b2, perm, layer):
    a, d = xs.shape
    de = w2.shape[2]
    n_steps = steps[0].shape[0]
    wspec = lambda r, c: pl.BlockSpec((None, None, r, c), lambda i, sg, st, lo, hi: (layer, sg[i], 0, 0))
    rows = pl.BlockSpec((MOE_BLOCK, d), lambda i, sg, st, lo, hi: (st[i], 0))
    grid_spec = pltpu.PrefetchScalarGridSpec(
        num_scalar_prefetch=4, grid=(n_steps,),
        in_specs=[rows, wspec(d, 2 * de), wspec(1, de), wspec(1, de), wspec(de, d), wspec(1, d),
                  pl.BlockSpec(perm.shape, lambda i, sg, st, lo, hi: (0, 0))],
        out_specs=rows,
        scratch_shapes=[pltpu.VMEM((d, de), BF), pltpu.VMEM((d, de), BF), pltpu.VMEM((de, d), BF)])
    return pl.pallas_call(
        _expert_kernel, grid_spec=grid_spec, out_shape=jax.ShapeDtypeStruct((a, d), BF),
        compiler_params=pltpu.CompilerParams(dimension_semantics=("arbitrary",),
                                             vmem_limit_bytes=VMEM_LIMIT_EXPERTS), name="moe_experts",
    )(*steps, xs, w1, b1g, b1l, w2, b2, perm)


def _router_kernel(x_ref, wh_ref, wl_ref, b_ref, tri_ref, idx_ref, gate_ref, rank_ref, cnt_ref, carry_ref):
    @pl.when(pl.program_id(0) == 0)
    def _():
        carry_ref[...] = jnp.zeros_like(carry_ref)

    x = x_ref[...]
    xh = x.astype(BF)
    xl = (x - xh.astype(F32)).astype(BF)
    wh = wh_ref[...]
    logits = (jnp.dot(xh, wh, preferred_element_type=F32) + jnp.dot(xl, wh, preferred_element_type=F32)
              + jnp.dot(xh, wl_ref[...], preferred_element_type=F32)) + b_ref[...]
    lane = lax.broadcasted_iota(jnp.int32, logits.shape, 1).astype(F32)
    cur = logits
    vals, hots = [], []
    for k in range(TOP_K):
        m = jnp.max(cur, axis=-1, keepdims=True)
        idx = jnp.min(jnp.where(cur == m, lane, float(LANES)), axis=-1, keepdims=True)
        hot = lane == idx
        cur = jnp.where(hot, NEG_INF, cur)
        vals.append(m)
        hots.append(hot.astype(F32))
        idx_ref[:, k:k + 1] = idx.astype(jnp.int32)
    ex = [jnp.exp(v - vals[0]) for v in vals]
    tot = ex[0]
    for e in ex[1:]:
        tot = tot + e
    for k in range(TOP_K):
        gate_ref[:, k:k + 1] = ex[k] / tot
    chosen = hots[0]
    for h in hots[1:]:
        chosen = chosen + h
    before = jnp.dot(tri_ref[...], chosen.astype(BF), preferred_element_type=F32) + carry_ref[...]
    for k in range(TOP_K):
        rank_ref[:, k:k + 1] = jnp.sum(hots[k] * before, axis=-1, keepdims=True).astype(jnp.int32)
    carry_ref[...] = carry_ref[...] + jnp.sum(chosen, axis=0, keepdims=True)
    cnt_ref[...] = carry_ref[...].astype(jnp.int32)


def _router_call(x, router_w, router_b, tm=512):
    n, d = x.shape
    ne = router_w.shape[1]
    assert ne <= LANES
    w = jnp.pad(router_w, ((0, 0), (0, LANES - ne)))
    wh = w.astype(BF)
    wl = (w - wh.astype(F32)).astype(BF)
    b = jnp.pad(router_b, (0, LANES - ne), constant_values=NEG_INF).reshape(1, LANES)
    tri = jnp.asarray(np.tril(np.ones((tm, tm), np.float32), -1)).astype(BF)
    col = lambda dt: jax.ShapeDtypeStruct((n, TOP_K), dt)
    cspec = pl.BlockSpec((tm, TOP_K), lambda i: (i, 0))
    wspec = pl.BlockSpec((d, LANES), lambda i: (0, 0))
    return pl.pallas_call(
        _router_kernel, grid=(n // tm,),
        in_specs=[pl.BlockSpec((tm, d), lambda i: (i, 0)), wspec, wspec,
                  pl.BlockSpec((1, LANES), lambda i: (0, 0)), pl.BlockSpec((tm, tm), lambda i: (0, 0))],
        out_specs=[cspec, cspec, cspec, pl.BlockSpec((1, LANES), lambda i: (0, 0))],
        out_shape=[col(jnp.int32), col(F32), col(jnp.int32), jax.ShapeDtypeStruct((1, LANES), jnp.int32)],
        scratch_shapes=[pltpu.VMEM((1, LANES), F32)],
        compiler_params=_cparams("arbitrary"), name="moe_router",
    )(x, wh, wl, b, tri)


def _route(x, router_w, router_b, n_experts):
    n = x.shape[0]
    top_i, gate, rank, cnt = _router_call(x, router_w, router_b)
    a = n * TOP_K
    assert a % MOE_BLOCK == 0 and n_experts * a < 2 ** 31
    e_flat = top_i.reshape(-1)
    ids = jnp.arange(a, dtype=jnp.int32)
    skey = jnp.sort(e_flat * a + ids)
    tok_sorted = (skey % a) // TOP_K
    experts = jnp.arange(n_experts, dtype=jnp.int32)
    counts = cnt[0, :n_experts]
    ends = jnp.cumsum(counts)
    starts = ends - counts
    pos = rank + jnp.sum(jnp.where(top_i[:, :, None] == experts[None, None, :], starts[None, None, :], 0), axis=-1)
    n_tiles = a // MOE_BLOCK
    first_tile = starts // MOE_BLOCK
    ntile = jnp.where(counts > 0, (ends - 1) // MOE_BLOCK - first_tile + 1, 0)
    cum = jnp.cumsum(ntile)
    n_steps = n_tiles + n_experts - 1
    s = jnp.arange(n_steps, dtype=jnp.int32)
    g = jnp.minimum(jnp.sum((s[:, None] >= cum[None, :]).astype(jnp.int32), axis=1), n_experts - 1)
    live = s < cum[-1]
    pick = g[:, None] == experts[None, :]
    at_g = lambda v: jnp.sum(jnp.where(pick, v[None, :], 0), axis=1)
    tile = jnp.where(live, at_g(first_tile) + s - at_g(cum - ntile), n_tiles - 1)
    lo = jnp.clip(at_g(starts) - tile * MOE_BLOCK, 0, MOE_BLOCK)
    hi = jnp.where(live, jnp.clip(at_g(ends) - tile * MOE_BLOCK, 0, MOE_BLOCK), lo)
    g = jnp.where(live, g, jnp.max(jnp.where(live, g, 0)))
    steps = tuple(v.astype(jnp.int32) for v in (g, tile, lo, hi))
    return gate, pos.reshape(n, TOP_K), tok_sorted, steps


def kernel(x, mem, ln_in_g, ln_in_b, w_in, b_gate, hy_conv_w, hy_conv_b, hy_f1_w, hy_f1_b, hy_f2_w, hy_f2_b, hy_f3_w, hy_freq, hy_skip, sw_sink, mem_w_kv, w_br_hy, w_br_swa, w_br_mem, w_out, ln1_g, ln1_b, router_w, router_b, moe_w1, moe_b1, moe_w2, moe_b2, ln2_g, ln2_b):
    batch, seq, d = x.shape
    depth = w_in.shape[0]
    n = batch * seq
    width = hy_skip.shape[2]
    hy_cols = 3 * width
    q_cols = SW_HEADS * SW_HEAD_DIM
    kv_cols = SW_KV_HEADS * SW_HEAD_DIM
    mq_cols = MEM_HEADS * MEM_HEAD_DIM
    gate_cols = N_BRANCH * d
    mem_tokens = mem.shape[1]
    n_experts = router_w.shape[2]
    alpha = (2 * depth) ** 0.25
    assert batch == 2, "the long convolution packs exactly two batch elements into one complex signal"
    assert w_in.shape[2] == hy_cols + q_cols + 2 * kv_cols + mq_cols + gate_cols
    n1 = 2 * seq // DFT_N2

    cst = _dft_consts(n1, DFT_N2)
    zfeat = _filter_features(seq)
    cos_t, sin_t = _rope_tables(seq)
    swap = _rope_swap_matrix()

    o_q = hy_cols
    o_k = o_q + q_cols
    o_v = o_k + kv_cols
    o_mq = o_v + kv_cols
    o_g = o_mq + mq_cols
    dup = lambda w0: jnp.concatenate(
        [w_in[:, :, w0 + h * SW_HEAD_DIM:w0 + (h + 1) * SW_HEAD_DIM] for h in range(SW_KV_HEADS) for _ in range(2)],
        axis=2)
    w_proj = jnp.concatenate([w_in[:, :, o_g:o_g + gate_cols], w_in[:, :, :hy_cols], w_in[:, :, o_q:o_q + q_cols],
                              w_in[:, :, o_mq:o_mq + mq_cols], dup(o_k), dup(o_v)], axis=2).astype(BF)
    kw = 2 * kv_cols
    c_hy, c_q, c_mq = gate_cols, gate_cols + hy_cols, gate_cols + hy_cols + q_cols
    c_k = c_mq + mq_cols
    c_v = c_k + kw
    proj_cols = c_v + kw
    tn_proj = proj_cols // 4

    de = moe_w2.shape[2]
    b1g = moe_b1[..., 0::2].reshape(depth, n_experts, 1, de)
    b1l = moe_b1[..., 1::2].reshape(depth, n_experts, 1, de)
    b2 = moe_b2.reshape(depth, n_experts, 1, d)
    perm = _deinterleave_matrix()
    mem_bf = mem.reshape(batch * mem_tokens, d).astype(BF)

    xf, xb = _ln_call(x.reshape(n, d), ln_in_g, ln_in_b)
    for l in range(depth):
        proj = _mm_call(xb, w_proj[l], BF, 1024, tn_proj, "in_proj")

        hv, hx1, hx2 = _shortconv_call(proj, hy_conv_w[l], hy_conv_b[l], batch, seq, c_hy // hy_cols)
        af = _hyena_filter_stage1(seq, width, hy_f1_w[l], hy_f1_b[l], hy_f2_w[l], hy_f2_b[l],
                                  hy_f3_w[l], hy_freq[l], zfeat, cst)
        wide = (2, n1 // 2, DFT_N2 * width)
        y_hy = hv.reshape(wide)
        for o, gx in enumerate((hx1, hx2)):
            y_hy = _hyena_conv(y_hy, gx.reshape(wide), af[o], hy_skip[l, o], cst, seq, width)
        y_hy = y_hy.reshape(n, width)

        q_rot, k_rot = _rope_call(proj, cos_t, sin_t, swap, batch, seq, c_q // q_cols, c_k // kw, q_cols, kw)
        y_sw = _attn_call(q_rot, k_rot, proj[:, c_v:c_v + kw].T, sw_sink[l], batch, seq)

        memkv = _mm_call(mem_bf, mem_w_kv[l].astype(BF), BF, mem_tokens, 2 * mq_cols, "mem_kv")
        y_mem = _memattn_call(proj, c_mq // mq_cols, memkv, batch, seq, mem_tokens)

        xf, xb = _merge_call(xf, proj, 0, b_gate[l], y_hy, y_sw, y_mem, w_br_hy[l].astype(BF),
                             w_br_swa[l].astype(BF), w_br_mem[l].astype(BF), w_out[l].astype(BF),
                             ln1_g[l], ln1_b[l], alpha)

        gate, dest, tok_sorted, steps = _route(xf, router_w[l], router_b[l], n_experts)
        xs = xb.at[tok_sorted].get(mode="promise_in_bounds")
        ys = _expert_call(steps, xs, moe_w1, b1g, b1l, moe_w2, b2, perm, l)
        yg = ys.at[dest.T].get(mode="promise_in_bounds")
        xf, xb = _combine_ln_call(xf, yg, gate, ln2_g[l], ln2_b[l], alpha)
    return xf.reshape(batch, seq, d)
```

```python
import functools
import math

import numpy as np
import jax
import jax.numpy as jnp
from jax import lax
from jax.experimental import pallas as pl
from jax.experimental.pallas import tpu as pltpu

BF = jnp.bfloat16
F32 = jnp.float32

HY_ORDER = 2
HY_SHORT = 3
HY_EMB = 33
HY_BANDS = (HY_EMB - 1) // 2
HY_FAST_DECAY = 0.3
HY_SLOW_DECAY = 1.5
HY_DECAY_TARGET = 1e-2
SW_HEADS = 8
SW_KV_HEADS = 2
SW_HEAD_DIM = 64
SW_WINDOW = 128
ROPE_THETA = 500000.0
ROPE_DIM = SW_HEAD_DIM // 4
MEM_HEADS = 4
MEM_HEAD_DIM = 128
N_BRANCH = 3
TOP_K = 4
SWIGLU_ALPHA = 1.702
SWIGLU_LIMIT = 7.0
MOE_BLOCK = 512
LN_EPS = 1e-5
NEG_INF = -1e30

LANES = 128
V7X_VMEM_BYTES = 64 * 1024 * 1024
VMEM_LIMIT = 48 * 1024 * 1024
VMEM_LIMIT_EXPERTS = 58 * 1024 * 1024
DFT_N2 = 128
SUBLANES = 8
FEAT_ROWS = -(-HY_EMB // SUBLANES) * SUBLANES


def _cparams(*sem):
    return pltpu.CompilerParams(dimension_semantics=sem, vmem_limit_bytes=VMEM_LIMIT)


def _ln_rows(r, g, b):
    mu = jnp.mean(r, axis=-1, keepdims=True)
    d = r - mu
    var = jnp.mean(d * d, axis=-1, keepdims=True)
    return d * lax.rsqrt(var + LN_EPS) * g + b


def _ln_kernel(x_ref, g_ref, b_ref, y_ref, yb_ref):
    y = _ln_rows(x_ref[...], g_ref[...], b_ref[...])
    y_ref[...] = y
    yb_ref[...] = y.astype(BF)


def _ln_call(x, g, b, tm=512):
    n, d = x.shape
    row = pl.BlockSpec((tm, d), lambda i: (i, 0))
    vec = pl.BlockSpec((1, d), lambda i: (0, 0))
    return pl.pallas_call(
        _ln_kernel, grid=(n // tm,), in_specs=[row, vec, vec], out_specs=[row, row],
        out_shape=[jax.ShapeDtypeStruct((n, d), F32), jax.ShapeDtypeStruct((n, d), BF)],
        compiler_params=_cparams("parallel"), name="ln_entry",
    )(x, g.reshape(1, d), b.reshape(1, d))


def _combine_ln_kernel(x_ref, y_ref, w_ref, g_ref, b_ref, o_ref, ob_ref, *, alpha):
    w = w_ref[...]
    f = y_ref[0].astype(F32) * w[:, 0:1]
    for k in range(1, TOP_K):
        f = f + y_ref[k].astype(F32) * w[:, k:k + 1]
    y = _ln_rows(alpha * x_ref[...] + f, g_ref[...], b_ref[...])
    o_ref[...] = y
    ob_ref[...] = y.astype(BF)


def _combine_ln_call(x, yg, w, g, b, alpha, tm=512):
    n, d = x.shape
    row = pl.BlockSpec((tm, d), lambda i: (i, 0))
    vec = pl.BlockSpec((1, d), lambda i: (0, 0))
    return pl.pallas_call(
        functools.partial(_combine_ln_kernel, alpha=alpha), grid=(n // tm,),
        in_specs=[row, pl.BlockSpec((TOP_K, tm, d), lambda i: (0, i, 0)),
                  pl.BlockSpec((tm, TOP_K), lambda i: (i, 0)), vec, vec],
        out_specs=[row, row],
        out_shape=[jax.ShapeDtypeStruct((n, d), F32), jax.ShapeDtypeStruct((n, d), BF)],
        compiler_params=_cparams("parallel"), name="moe_combine_ln",
    )(x, yg, w, g.reshape(1, d), b.reshape(1, d))


def _mm_kernel(a_ref, b_ref, o_ref):
    o_ref[...] = jnp.dot(a_ref[...], b_ref[...], preferred_element_type=F32).astype(o_ref.dtype)


def _mm_call(a, b, out_dtype, tm, tn, name):
    m, k = a.shape
    n = b.shape[1]
    return pl.pallas_call(
        _mm_kernel, grid=(n // tn, m // tm),
        in_specs=[pl.BlockSpec((tm, k), lambda j, i: (i, 0)), pl.BlockSpec((k, tn), lambda j, i: (0, j))],
        out_specs=pl.BlockSpec((tm, tn), lambda j, i: (i, j)),
        out_shape=jax.ShapeDtypeStruct((m, n), out_dtype),
        compiler_params=_cparams("parallel", "parallel"), name=name,
    )(a, b)


def _shortconv_kernel(zp_ref, z_ref, zn_ref, w_ref, b_ref, hv_ref, hx1_ref, hx2_ref, *, tl, halo, width):
    i = pl.program_id(1)
    nt = pl.num_programs(1)
    z = z_ref[...].astype(F32)
    prev_row = zp_ref[halo - 1:halo, :].astype(F32)
    next_row = zn_ref[0:1, :].astype(F32)
    prev_row = jnp.where(i == 0, 0.0, prev_row)
    next_row = jnp.where(i == nt - 1, 0.0, next_row)
    row = lax.broadcasted_iota(jnp.int32, z.shape, 0)
    zm1 = jnp.where(row == 0, prev_row, pltpu.roll(z, 1, 0))
    zp1 = jnp.where(row == tl - 1, next_row, pltpu.roll(z, tl - 1, 0))
    w = w_ref[...]
    out = zm1 * w[0:1] + b_ref[...] + z * w[1:2] + zp1 * w[2:3]
    hv_ref[...] = out[:, 0:width].astype(BF)
    hx1_ref[...] = out[:, width:2 * width].astype(BF)
    hx2_ref[...] = out[:, 2 * width:3 * width].astype(BF)


def _shortconv_call(proj, conv_w, conv_b, batch, seq, col_blk, tl=512, halo=16):
    n = proj.shape[0]
    c3 = conv_w.shape[1]
    width = c3 // 3
    nt = seq // tl
    hb = tl // halo
    nhb = seq // halo
    main = pl.BlockSpec((tl, c3), lambda b, i: (b * nt + i, col_blk))
    prev = pl.BlockSpec((halo, c3), lambda b, i: (b * nhb + jnp.maximum(i * hb - 1, 0), col_blk))
    nxt = pl.BlockSpec((halo, c3), lambda b, i: (b * nhb + jnp.minimum((i + 1) * hb, nhb - 1), col_blk))
    out = pl.BlockSpec((tl, width), lambda b, i: (b * nt + i, 0))
    return pl.pallas_call(
        functools.partial(_shortconv_kernel, tl=tl, halo=halo, width=width), grid=(batch, nt),
        in_specs=[prev, main, nxt, pl.BlockSpec((HY_SHORT, c3), lambda b, i: (0, 0)),
                  pl.BlockSpec((1, c3), lambda b, i: (0, 0))],
        out_specs=[out, out, out],
        out_shape=[jax.ShapeDtypeStruct((n, width), BF)] * 3,
        compiler_params=_cparams("parallel", "parallel"), name="hy_shortconv",
    )(proj, proj, proj, conv_w, conv_b.reshape(1, c3))


def _filter_kernel(z_ref, f1w_ref, f1b_ref, f2w_ref, f2b_ref, f3w_ref, freq_ref, adel_ref, k_ref,
                   *, tr, seq, n_inner):
    i = pl.program_id(0)
    hp = lax.Precision.HIGHEST
    fr = freq_ref[...]
    h = jnp.sin(fr * (jnp.dot(f1w_ref[...], z_ref[...], precision=hp, preferred_element_type=F32)
                      + f1b_ref[...]))
    for j in range(n_inner):
        h = jnp.sin(fr * (jnp.dot(f2w_ref[j], h, precision=hp, preferred_element_type=F32) + f2b_ref[j]))
    k = lax.dot_general(h.astype(BF), f3w_ref[...], (((0,), (0,)), ((), ())),
                        preferred_element_type=F32)
    width = adel_ref.shape[1]
    row = i * tr + lax.broadcasted_iota(jnp.int32, (tr, width), 0)
    pos = jnp.where(row < seq, row, 2 * seq - row)
    win = jnp.exp(-(pos.astype(F32) * (1.0 / (seq - 1))) * adel_ref[...])
    win = jnp.where(row == seq, 0.0, win)
    for o in range(HY_ORDER):
        k_ref[o] = (k[:, o * width:(o + 1) * width] * win).astype(k_ref.dtype)


def _filter_call(zfeat_t, f1w_t, f1b, f2w_t, f2b, f3w_dir, freq, adel, seq, tr=1024):
    zr, n2l = zfeat_t.shape
    hid = f1w_t.shape[0]
    n_inner = f2w_t.shape[0]
    ow = f3w_dir.shape[2]
    width = adel.shape[1]
    half = (n2l // tr) // 2
    full = lambda *shape: pl.BlockSpec(shape, lambda i: (0,) * len(shape))
    return pl.pallas_call(
        functools.partial(_filter_kernel, tr=tr, seq=seq, n_inner=n_inner), grid=(n2l // tr,),
        in_specs=[pl.BlockSpec((zr, tr), lambda i: (0, i)), full(hid, zr), full(hid, 1),
                  full(n_inner, hid, hid), full(n_inner, hid, 1),
                  pl.BlockSpec((None, hid, ow), lambda i: (i // half, 0, 0)),
                  full(hid, 1), full(1, width)],
        out_specs=pl.BlockSpec((HY_ORDER, tr, width), lambda i: (0, i, 0)),
        out_shape=jax.ShapeDtypeStruct((HY_ORDER, n2l, width), BF),
        compiler_params=_cparams("parallel"), name="hy_filter",
    )(zfeat_t, f1w_t, f1b, f2w_t, f2b, f3w_dir, freq, adel)


def _dft_consts(n1, n2):
    n = n1 * n2
    h1 = n1 // 2
    k1 = np.arange(n1)[:, None]
    ang = -2.0 * np.pi * (k1 * np.arange(h1)[None, :]) / n1
    cr, ci = np.cos(ang), np.sin(ang)
    e1 = np.block([[cr, -ci], [ci, cr]])
    angf = -2.0 * np.pi * (k1 * np.arange(n1)[None, :]) / n1
    ef = np.concatenate([np.cos(angf), np.sin(angf)], axis=0)
    a2 = -2.0 * np.pi * (np.arange(n2)[:, None] * np.arange(n2)[None, :]) / n2
    f2r, f2i = np.cos(a2), np.sin(a2)
    dm = np.block([[f2r, f2i], [-f2i, f2r]])
    at = -2.0 * np.pi * (np.arange(n1)[:, None] * np.arange(n2)[None, :]) / n
    twr, twi = np.cos(at), np.sin(at)
    ab = 2.0 * np.pi * (np.arange(h1)[:, None] * np.arange(n1)[None, :]) / n1
    br, bi = np.cos(ab) / n, np.sin(ab) / n
    f = lambda a: jnp.asarray(a, dtype=F32)
    return dict(e1=f(e1), ef=f(ef), f2r=f(f2r), f2i=f(f2i), dm=f(dm),
                twr=f(twr.reshape(n1, 1, n2)), twi=f(twi.reshape(n1, 1, n2)),
                ctr=f(twr.T.reshape(n2, 1, n1)), cti=f(-twi.T.reshape(n2, 1, n1)),
                br=f(br), bi=f(bi))


def _dft_s1_kernel(e_ref, top_ref, bot_ref, a_ref):
    rhs = jnp.concatenate([top_ref[...], bot_ref[...]], axis=0).astype(BF)
    res = jnp.dot(e_ref[...], rhs, preferred_element_type=F32)
    n1 = a_ref.shape[1]
    a_ref[0] = res[:n1].astype(a_ref.dtype)
    a_ref[1] = res[n1:].astype(a_ref.dtype)


def _dft_s1_call(e_mat, x3, top_idx, bot_idx, n1, tn=2048):
    h1, cols = x3.shape[1], x3.shape[2]
    return pl.pallas_call(
        _dft_s1_kernel, grid=(cols // tn,),
        in_specs=[pl.BlockSpec((2 * n1, 2 * h1), lambda j: (0, 0)),
                  pl.BlockSpec((None, h1, tn), lambda j: (top_idx, 0, j)),
                  pl.BlockSpec((None, h1, tn), lambda j: (bot_idx, 0, j))],
        out_specs=pl.BlockSpec((2, n1, tn), lambda j: (0, 0, j)),
        out_shape=jax.ShapeDtypeStruct((2, n1, cols), BF),
        compiler_params=_cparams("parallel"), name="hy_dft_stage1",
    )(e_mat.astype(BF), x3, x3)


def _twiddled_f2(f2r, f2i, tr, ti):
    gr = f2r * tr - f2i * ti
    gi = f2r * ti + f2i * tr
    return jnp.concatenate([jnp.concatenate([gr, -gi], axis=1), jnp.concatenate([gi, gr], axis=1)], axis=0)


def _dft_s2_conv_kernel(a_ref, af_ref, f2r_ref, f2i_ref, dm_ref, twr_ref, twi_ref, b_ref, *, kb):
    n2 = f2r_ref.shape[0]
    dm = dm_ref[...]
    for j in range(kb):
        gm = _twiddled_f2(f2r_ref[...], f2i_ref[...], twr_ref[j], twi_ref[j]).astype(BF)
        rhs = jnp.concatenate([a_ref[0, j], a_ref[1, j]], axis=0)
        x = jnp.dot(gm, rhs, preferred_element_type=F32)
        kf = jnp.dot(gm, jnp.concatenate([af_ref[0, j], af_ref[1, j]], axis=0),
                     preferred_element_type=F32)
        xr, xi = x[:n2], x[n2:]
        kr, ki = kf[:n2], kf[n2:]
        y = jnp.concatenate([xr * kr - xi * ki, xr * ki + xi * kr], axis=0).astype(BF)
        bm = jnp.dot(dm, y, preferred_element_type=F32)
        b_ref[0, j] = bm[:n2].astype(b_ref.dtype)
        b_ref[1, j] = bm[n2:].astype(b_ref.dtype)


def _dft_s2_conv_call(a4, af4, cst, kb=8):
    _, n1, n2, c = a4.shape
    blk = pl.BlockSpec((2, kb, n2, c), lambda i: (0, i, 0, 0))
    sq = pl.BlockSpec((n2, n2), lambda i: (0, 0))
    tw = pl.BlockSpec((kb, 1, n2), lambda i: (i, 0, 0))
    return pl.pallas_call(
        functools.partial(_dft_s2_conv_kernel, kb=kb), grid=(n1 // kb,),
        in_specs=[blk, blk, sq, sq, pl.BlockSpec((2 * n2, 2 * n2), lambda i: (0, 0)), tw, tw],
        out_specs=blk, out_shape=jax.ShapeDtypeStruct(a4.shape, BF),
        compiler_params=_cparams("parallel"), name="hy_dft_stage2_conv",
    )(a4, af4, cst["f2r"], cst["f2i"], cst["dm"].astype(BF), cst["twr"], cst["twi"])


def _dft_s1inv_kernel(b_ref, br_ref, bi_ref, ctr_ref, cti_ref, u_ref, gx_ref, skip_ref, o_ref, *, ng, c):
    h1 = br_ref.shape[0]
    skip = skip_ref[...]
    for j in range(ng):
        tr, ti = ctr_ref[j], cti_ref[j]
        mr = br_ref[...] * tr - bi_ref[...] * ti
        mi = br_ref[...] * ti + bi_ref[...] * tr
        em = jnp.concatenate([jnp.concatenate([mr, -mi], axis=1),
                              jnp.concatenate([mi, mr], axis=1)], axis=0).astype(BF)
        y = jnp.dot(em, b_ref[:, j * c:(j + 1) * c], preferred_element_type=F32)
        for b in range(2):
            u = u_ref[b, :, j * c:(j + 1) * c].astype(F32)
            g = gx_ref[b, :, j * c:(j + 1) * c].astype(F32)
            o_ref[b, :, j * c:(j + 1) * c] = (g * (y[b * h1:(b + 1) * h1] + skip * u)).astype(o_ref.dtype)


def _dft_s1inv_call(b2, u3, gx3, skip, cst, c, ng=4):
    rows, cols = b2.shape
    h1 = u3.shape[1]
    n2 = cols // c
    n1 = rows // 2
    ub = pl.BlockSpec((2, h1, ng * c), lambda i: (0, 0, i))
    cb = pl.BlockSpec((h1, n1), lambda i: (0, 0))
    tw = pl.BlockSpec((ng, 1, n1), lambda i: (i, 0, 0))
    return pl.pallas_call(
        functools.partial(_dft_s1inv_kernel, ng=ng, c=c), grid=(n2 // ng,),
        in_specs=[pl.BlockSpec((rows, ng * c), lambda i: (0, i)), cb, cb, tw, tw, ub, ub,
                  pl.BlockSpec((1, c), lambda i: (0, 0))],
        out_specs=ub, out_shape=jax.ShapeDtypeStruct(u3.shape, BF),
        compiler_params=_cparams("parallel"), name="hy_dft_stage1_inverse",
    )(b2, cst["br"], cst["bi"], cst["ctr"], cst["cti"], u3, gx3, skip.reshape(1, c))


def _filter_features(seq):
    t01 = jnp.linspace(0.0, 1.0, seq, dtype=F32)[:, None]
    w = (2.0 * math.pi) * jnp.arange(seq, dtype=F32)[:, None] / seq
    bands = jnp.linspace(1e-4, HY_BANDS - 1, HY_BANDS, dtype=F32)
    z = jnp.concatenate([t01, jnp.cos(bands * w), -jnp.sin(bands * w)], axis=-1)
    z2 = jnp.concatenate([z, z[:1], z[1:][::-1]], axis=0)
    return jnp.pad(z2, ((0, 0), (0, FEAT_ROWS - HY_EMB))).T


def _hyena_filter_stage1(seq, width, f1_w, f1_b, f2_w, f2_b, f3_w, freq, zfeat_t, cst):
    hid = f1_w.shape[1]
    n1 = 2 * seq // DFT_N2
    max_decay = math.log(HY_DECAY_TARGET) / HY_FAST_DECAY
    min_decay = math.log(HY_DECAY_TARGET) / HY_SLOW_DECAY
    adel = jnp.abs(jnp.linspace(min_decay, max_decay, width, dtype=F32)).reshape(1, width)
    f1w_t = jnp.pad(f1_w, ((0, FEAT_ROWS - HY_EMB), (0, 0))).T
    f3d = f3_w.reshape(hid, HY_ORDER, 2, width).transpose(2, 0, 1, 3).reshape(2, hid, HY_ORDER * width)
    k3 = _filter_call(zfeat_t, f1w_t, f1_b.reshape(hid, 1), f2_w.transpose(0, 2, 1), f2_b.reshape(-1, hid, 1),
                      f3d.astype(BF), freq.reshape(hid, 1), adel, seq)
    kw = k3.reshape(HY_ORDER * 2, n1 // 2, DFT_N2 * width)
    return [_dft_s1_call(cst["ef"], kw, 2 * o, 2 * o + 1, n1).reshape(2, n1, DFT_N2, width)
            for o in range(HY_ORDER)]


def _hyena_conv(u3, gx3, af, skip, cst, seq, width):
    n1 = 2 * seq // DFT_N2
    a = _dft_s1_call(cst["e1"], u3, 0, 1, n1)
    bm = _dft_s2_conv_call(a.reshape(2, n1, DFT_N2, width), af, cst)
    return _dft_s1inv_call(bm.reshape(2 * n1, DFT_N2 * width), u3, gx3, skip, cst, width)


def _rope_tables(seq):
    half = ROPE_DIM // 2
    inv = jnp.power(jnp.float32(ROPE_THETA), -jnp.arange(half, dtype=F32) * (2.0 / ROPE_DIM))
    ang = jnp.arange(seq, dtype=jnp.int32).astype(F32)[:, None] * inv
    cos, sin = jnp.cos(ang), jnp.sin(ang)
    rest = SW_HEAD_DIM - ROPE_DIM
    ones = jnp.ones((seq, rest), F32)
    zeros = jnp.zeros((seq, rest), F32)
    ch = jnp.concatenate([cos, cos, ones], axis=1)
    sh = jnp.concatenate([-sin, sin, zeros], axis=1)
    reps = LANES // SW_HEAD_DIM
    return jnp.tile(ch, (1, reps)), jnp.tile(sh, (1, reps))


def _rope_swap_matrix():
    half = ROPE_DIM // 2
    p = np.zeros((LANES, LANES), np.float32)
    for j in range(LANES):
        d = j % SW_HEAD_DIM
        if d < half:
            p[j + half, j] = 1.0
        elif d < ROPE_DIM:
            p[j - half, j] = 1.0
    return jnp.asarray(p)


def _attn_kernel(sink_ref, q_ref, kp_ref, km_ref, kn_ref, vp_ref, vm_ref, vn_ref, cp_ref, cm_ref, cn_ref,
                 sp_ref, sm_ref, sn_ref, swap_ref, o_ref, *, tq, seq):
    blk = SW_WINDOW
    i = pl.program_id(1)
    swap = swap_ref[...]

    def rot(x, c, s):
        return x.astype(F32) * c + jnp.dot(x, swap, preferred_element_type=F32) * s

    cwin = jnp.concatenate([cp_ref[...], cm_ref[...], cn_ref[...]], axis=0)
    swin = jnp.concatenate([sp_ref[...], sm_ref[...], sn_ref[...]], axis=0)
    kraw = jnp.concatenate([kp_ref[...], km_ref[...], kn_ref[...]], axis=0)
    kwin = jnp.concatenate([rot(kraw[:, h * LANES:(h + 1) * LANES], cwin, swin).astype(BF)
                            for h in range(SW_KV_HEADS)], axis=1)
    q_scale = SW_HEAD_DIM ** -0.5
    qrot = [(rot(q_ref[:, g * LANES:(g + 1) * LANES], cm_ref[...], sm_ref[...]) * q_scale).astype(BF)
            for g in range(q_ref.shape[1] // LANES)]
    vwin_t = jnp.concatenate([vp_ref[...], vm_ref[...], vn_ref[...]], axis=1)
    lo = lax.broadcasted_iota(jnp.int32, (blk, LANES), 1) < SW_HEAD_DIM
    zero = jnp.zeros((), BF)
    per_group = LANES // SW_HEAD_DIM
    groups_per_kv = (SW_HEADS // SW_KV_HEADS) // per_group
    stack = groups_per_kv * per_group
    w_idx = lax.broadcasted_iota(jnp.int32, (3 * blk, stack * blk), 0)
    a_idx = lax.broadcasted_iota(jnp.int32, (3 * blk, stack * blk), 1) & (blk - 1)
    rel = w_idx - a_idx
    for jb in range(tq // blk):
        kpos = i * tq + (jb - 1) * blk + w_idx
        bad = (rel < 0) | (rel > 2 * SW_WINDOW) | (kpos < 0) | (kpos >= seq)
        for h in range(SW_KV_HEADS):
            parts, sinks = [], []
            for gg in range(groups_per_kv):
                g = h * groups_per_kv + gg
                qg = qrot[g][jb * blk:(jb + 1) * blk]
                parts += [jnp.where(lo, qg, zero), jnp.where(lo, zero, qg)]
                sinks += [jnp.full((1, blk), sink_ref[g * per_group + par], F32) for par in range(per_group)]
            qs = jnp.concatenate(parts, axis=0)
            sk = jnp.concatenate(sinks, axis=1)
            kh = kwin[jb * blk:(jb + 3) * blk, h * LANES:(h + 1) * LANES]
            vt = vwin_t[h * LANES:(h + 1) * LANES, jb * blk:(jb + 3) * blk]
            s = lax.dot_general(kh, qs, (((1,), (1,)), ((), ())), preferred_element_type=F32)
            s = jnp.where(bad, NEG_INF, s)
            m = jnp.maximum(jnp.max(s, axis=0, keepdims=True), sk)
            p = jnp.exp(s - m)
            denom = jnp.sum(p, axis=0, keepdims=True) + jnp.exp(sk - m)
            ot = jnp.dot(vt, p.astype(BF), preferred_element_type=F32) / denom
            o = ot.T
            for gg in range(groups_per_kv):
                g = h * groups_per_kv + gg
                oa = o[(gg * per_group) * blk:(gg * per_group + 1) * blk]
                ob = o[(gg * per_group + 1) * blk:(gg * per_group + 2) * blk]
                o_ref[jb * blk:(jb + 1) * blk, g * LANES:(g + 1) * LANES] = jnp.where(lo, oa, ob).astype(BF)


def _attn_call(proj, q_blk, k_blk, qw, kw, v_t, cos_t, sin_t, swap, sink, batch, seq, tq=512):
    n = proj.shape[0]
    blk = SW_WINDOW
    nt = seq // tq
    r = tq // blk
    nb = seq // blk
    prev_p = lambda i: jnp.maximum(i * r - 1, 0)
    next_p = lambda i: jnp.minimum((i + 1) * r, nb - 1)
    tab = lambda rows, f: pl.BlockSpec((rows, LANES), lambda b, i, s: (f(i), 0))
    tabs = [tab(blk, prev_p), tab(tq, lambda i: i), tab(blk, next_p)]
    grid_spec = pltpu.PrefetchScalarGridSpec(
        num_scalar_prefetch=1, grid=(batch, nt),
        in_specs=[pl.BlockSpec((tq, qw), lambda b, i, s: (b * nt + i, q_blk)),
                  pl.BlockSpec((blk, kw), lambda b, i, s: (b * nb + prev_p(i), k_blk)),
                  pl.BlockSpec((tq, kw), lambda b, i, s: (b * nt + i, k_blk)),
                  pl.BlockSpec((blk, kw), lambda b, i, s: (b * nb + next_p(i), k_blk)),
                  pl.BlockSpec((kw, blk), lambda b, i, s: (0, b * nb + prev_p(i))),
                  pl.BlockSpec((kw, tq), lambda b, i, s: (0, b * nt + i)),
                  pl.BlockSpec((kw, blk), lambda b, i, s: (0, b * nb + next_p(i)))]
                 + tabs + tabs + [pl.BlockSpec((LANES, LANES), lambda b, i, s: (0, 0))],
        out_specs=pl.BlockSpec((tq, qw), lambda b, i, s: (b * nt + i, 0)))
    return pl.pallas_call(
        functools.partial(_attn_kernel, tq=tq, seq=seq), grid_spec=grid_spec,
        out_shape=jax.ShapeDtypeStruct((n, qw), BF),
        compiler_params=_cparams("parallel", "parallel"), name="sw_attention",
    )(sink, proj, proj, proj, proj, v_t, v_t, v_t, cos_t, cos_t, cos_t, sin_t, sin_t, sin_t, swap.astype(BF))


def _memattn_kernel(q_ref, mk_ref, mv_ref, o_ref):
    scale = MEM_HEAD_DIM ** -0.5
    for h in range(MEM_HEADS):
        sl = slice(h * MEM_HEAD_DIM, (h + 1) * MEM_HEAD_DIM)
        s = lax.dot_general(q_ref[:, sl], mk_ref[:, sl], (((1,), (1,)), ((), ())),
                            preferred_element_type=F32) * scale
        m = jnp.max(s, axis=-1, keepdims=True)
        p = jnp.exp(s - m)
        denom = jnp.sum(p, axis=-1, keepdims=True)
        o = jnp.dot(p.astype(BF), mv_ref[:, sl], preferred_element_type=F32) / denom
        o_ref[:, sl] = o.astype(BF)


def _memattn_call(proj, q_blk, memkv, batch, seq, mem_tokens, tm=512):
    n = proj.shape[0]
    w = MEM_HEADS * MEM_HEAD_DIM
    nt = seq // tm
    return pl.pallas_call(
        _memattn_kernel, grid=(batch, nt),
        in_specs=[pl.BlockSpec((tm, w), lambda b, i: (b * nt + i, q_blk)),
                  pl.BlockSpec((mem_tokens, w), lambda b, i: (b, 0)),
                  pl.BlockSpec((mem_tokens, w), lambda b, i: (b, 1))],
        out_specs=pl.BlockSpec((tm, w), lambda b, i: (b * nt + i, 0)),
        out_shape=jax.ShapeDtypeStruct((n, w), BF),
        compiler_params=_cparams("parallel", "parallel"), name="mem_attention",
    )(proj, memkv, memkv)


def _merge_kernel(x_ref, gl_ref, bg_ref, yh_ref, ys_ref, ym_ref, wh_ref, ws_ref, wm_ref, wo_ref,
                  g_ref, b_ref, o_ref, ob_ref, *, alpha, d):
    merged = None
    for br, (y_ref, w_ref) in enumerate(((yh_ref, wh_ref), (ys_ref, ws_ref), (ym_ref, wm_ref))):
        gate = jax.nn.sigmoid(gl_ref[:, br * d:(br + 1) * d].astype(F32) + bg_ref[br:br + 1, :])
        t = gate * jnp.dot(y_ref[...], w_ref[...], preferred_element_type=F32)
        merged = t if merged is None else merged + t
    h = jnp.dot(merged.astype(BF), wo_ref[...], preferred_element_type=F32)
    y = _ln_rows(alpha * x_ref[...] + h, g_ref[...], b_ref[...])
    o_ref[...] = y
    ob_ref[...] = y.astype(BF)


def _merge_call(x, proj, gate_blk, b_gate, y_hy, y_sw, y_mem, w_hy, w_sw, w_mem, w_out, g, b, alpha, tm=512):
    n, d = x.shape
    row = lambda w: pl.BlockSpec((tm, w), lambda i: (i, 0))
    full = lambda a: pl.BlockSpec(a.shape, lambda i: (0,) * a.ndim)
    g2, b2 = g.reshape(1, d), b.reshape(1, d)
    return pl.pallas_call(
        functools.partial(_merge_kernel, alpha=alpha, d=d), grid=(n // tm,),
        in_specs=[row(d), pl.BlockSpec((tm, N_BRANCH * d), lambda i: (i, gate_blk)), full(b_gate),
                  row(y_hy.shape[1]), row(y_sw.shape[1]), row(y_mem.shape[1]),
                  full(w_hy), full(w_sw), full(w_mem), full(w_out), full(g2), full(b2)],
        out_specs=[row(d), row(d)],
        out_shape=[jax.ShapeDtypeStruct((n, d), F32), jax.ShapeDtypeStruct((n, d), BF)],
        compiler_params=_cparams("parallel"), name="merge_outproj_ln",
    )(x, proj, b_gate, y_hy, y_sw, y_mem, w_hy, w_sw, w_mem, w_out, g2, b2)


def _deinterleave_matrix():
    perm = np.zeros((2 * LANES, 2 * LANES), np.float32)
    perm[2 * np.arange(LANES), np.arange(LANES)] = 1.0
    perm[2 * np.arange(LANES) + 1, LANES + np.arange(LANES)] = 1.0
    return jnp.asarray(perm).astype(BF)


def _expert_kernel(sg_ref, st_ref, lo_ref, hi_ref, x_ref, w1_ref, b1g_ref, b1l_ref, w2_ref, b2_ref, perm_ref,
                   y_ref, w1g_ref, w1l_ref, w2b_ref):
    s = pl.program_id(0)
    lo, hi = lo_ref[s], hi_ref[s]
    prev = jnp.maximum(s - 1, 0)

    @pl.when((s == 0) | (sg_ref[s] != sg_ref[prev]))
    def _():
        p = perm_ref[...]
        for c in range(w1g_ref.shape[1] // LANES):
            r = jnp.dot(w1_ref[:, 2 * c * LANES:2 * (c + 1) * LANES].astype(BF), p, preferred_element_type=F32)
            w1g_ref[:, c * LANES:(c + 1) * LANES] = r[:, :LANES].astype(BF)
            w1l_ref[:, c * LANES:(c + 1) * LANES] = r[:, LANES:].astype(BF)
        w2b_ref[...] = w2_ref[...].astype(BF)

    @pl.when(hi > lo)
    def _():
        x = x_ref[...]
        hg = jnp.dot(x, w1g_ref[...], preferred_element_type=F32) + b1g_ref[...]
        hl = jnp.dot(x, w1l_ref[...], preferred_element_type=F32) + b1l_ref[...]
        hg = jnp.minimum(hg, SWIGLU_LIMIT)
        hl = jnp.clip(hl, -SWIGLU_LIMIT, SWIGLU_LIMIT)
        act = hg * jax.nn.sigmoid(SWIGLU_ALPHA * hg) * (hl + 1.0)
        y = jnp.dot(act.astype(BF), w2b_ref[...], preferred_element_type=F32) + b2_ref[...]
        whole = (lo == 0) & (hi == y.shape[0])
        first_visit = (s == 0) | (st_ref[s] != st_ref[prev])

        @pl.when(whole)
        def _():
            y_ref[...] = y.astype(y_ref.dtype)

        def masked(prev):
            row = lax.broadcasted_iota(jnp.int32, y.shape, 0)
            y_ref[...] = jnp.where((row >= lo) & (row < hi), y, prev).astype(y_ref.dtype)

        @pl.when(jnp.logical_not(whole) & first_visit)
        def _():
            masked(jnp.zeros_like(y))

        @pl.when(jnp.logical_not(whole) & jnp.logical_not(first_visit))
        def _():
            masked(y_ref[...].astype(F32))


def _expert_call(steps, xs, w1, b1g, b1l, w2, b2, perm, layer):
    a, d = xs.shape
    de = w2.shape[2]
    n_steps = steps[0].shape[0]
    wspec = lambda r, c: pl.BlockSpec((None, None, r, c), lambda i, sg, st, lo, hi: (layer, sg[i], 0, 0))
    rows = pl.BlockSpec((MOE_BLOCK, d), lambda i, sg, st, lo, hi: (st[i], 0))
    grid_spec = pltpu.PrefetchScalarGridSpec(
        num_scalar_prefetch=4, grid=(n_steps,),
        in_specs=[rows, wspec(d, 2 * de), wspec(1, de), wspec(1, de), wspec(de, d), wspec(1, d),
                  pl.BlockSpec(perm.shape, lambda i, sg, st, lo, hi: (0, 0))],
        out_specs=rows,
        scratch_shapes=[pltpu.VMEM((d, de), BF), pltpu.VMEM((d, de), BF), pltpu.VMEM((de, d), BF)])
    return pl.pallas_call(
        _expert_kernel, grid_spec=grid_spec, out_shape=jax.ShapeDtypeStruct((a, d), BF),
        compiler_params=pltpu.CompilerParams(dimension_semantics=("arbitrary",),
                                             vmem_limit_bytes=VMEM_LIMIT_EXPERTS), name="moe_experts",
    )(*steps, xs, w1, b1g, b1l, w2, b2, perm)


def _router_kernel(x_ref, wh_ref, wl_ref, b_ref, tri_ref, idx_ref, gate_ref, rank_ref, cnt_ref, carry_ref):
    @pl.when(pl.program_id(0) == 0)
    def _():
        carry_ref[...] = jnp.zeros_like(carry_ref)

    x = x_ref[...]
    xh = x.astype(BF)
    xl = (x - xh.astype(F32)).astype(BF)
    wh = wh_ref[...]
    logits = (jnp.dot(xh, wh, preferred_element_type=F32) + jnp.dot(xl, wh, preferred_element_type=F32)
              + jnp.dot(xh, wl_ref[...], preferred_element_type=F32)) + b_ref[...]
    lane = lax.broadcasted_iota(jnp.int32, logits.shape, 1).astype(F32)
    cur = logits
    vals, hots = [], []
    for k in range(TOP_K):
        m = jnp.max(cur, axis=-1, keepdims=True)
        idx = jnp.min(jnp.where(cur == m, lane, float(LANES)), axis=-1, keepdims=True)
        hot = lane == idx
        cur = jnp.where(hot, NEG_INF, cur)
        vals.append(m)
        hots.append(hot.astype(F32))
        idx_ref[:, k:k + 1] = idx.astype(jnp.int32)
    ex = [jnp.exp(v - vals[0]) for v in vals]
    tot = ex[0]
    for e in ex[1:]:
        tot = tot + e
    for k in range(TOP_K):
        gate_ref[:, k:k + 1] = ex[k] / tot
    chosen = hots[0]
    for h in hots[1:]:
        chosen = chosen + h
    before = jnp.dot(tri_ref[...], chosen.astype(BF), preferred_element_type=F32) + carry_ref[...]
    for k in range(TOP_K):
        rank_ref[:, k:k + 1] = jnp.sum(hots[k] * before, axis=-1, keepdims=True).astype(jnp.int32)
    carry_ref[...] = carry_ref[...] + jnp.sum(chosen, axis=0, keepdims=True)
    cnt_ref[...] = carry_ref[...].astype(jnp.int32)


def _router_call(x, router_w, router_b, tm=512):
    n, d = x.shape
    ne = router_w.shape[1]
    assert ne <= LANES
    w = jnp.pad(router_w, ((0, 0), (0, LANES - ne)))
    wh = w.astype(BF)
    wl = (w - wh.astype(F32)).astype(BF)
    b = jnp.pad(router_b, (0, LANES - ne), constant_values=NEG_INF).reshape(1, LANES)
    tri = jnp.asarray(np.tril(np.ones((tm, tm), np.float32), -1)).astype(BF)
    col = lambda dt: jax.ShapeDtypeStruct((n, TOP_K), dt)
    cspec = pl.BlockSpec((tm, TOP_K), lambda i: (i, 0))
    wspec = pl.BlockSpec((d, LANES), lambda i: (0, 0))
    return pl.pallas_call(
        _router_kernel, grid=(n // tm,),
        in_specs=[pl.BlockSpec((tm, d), lambda i: (i, 0)), wspec, wspec,
                  pl.BlockSpec((1, LANES), lambda i: (0, 0)), pl.BlockSpec((tm, tm), lambda i: (0, 0))],
        out_specs=[cspec, cspec, cspec, pl.BlockSpec((1, LANES), lambda i: (0, 0))],
        out_shape=[col(jnp.int32), col(F32), col(jnp.int32), jax.ShapeDtypeStruct((1, LANES), jnp.int32)],
        scratch_shapes=[pltpu.VMEM((1, LANES), F32)],
        compiler_params=_cparams("arbitrary"), name="moe_router",
    )(x, wh, wl, b, tri)


def _route(x, router_w, router_b, n_experts):
    n = x.shape[0]
    top_i, gate, rank, cnt = _router_call(x, router_w, router_b)
    a = n * TOP_K
    assert a % MOE_BLOCK == 0 and n_experts * a < 2 ** 31
    e_flat = top_i.reshape(-1)
    ids = jnp.arange(a, dtype=jnp.int32)
    skey = jnp.sort(e_flat * a + ids)
    tok_sorted = (skey % a) // TOP_K
    experts = jnp.arange(n_experts, dtype=jnp.int32)
    counts = cnt[0, :n_experts]
    ends = jnp.cumsum(counts)
    starts = ends - counts
    pos = rank + jnp.sum(jnp.where(top_i[:, :, None] == experts[None, None, :], starts[None, None, :], 0), axis=-1)
    n_tiles = a // MOE_BLOCK
    first_tile = starts // MOE_BLOCK
    ntile = jnp.where(counts > 0, (ends - 1) // MOE_BLOCK - first_tile + 1, 0)
    cum = jnp.cumsum(ntile)
    n_steps = n_tiles + n_experts - 1
    s = jnp.arange(n_steps, dtype=jnp.int32)
    g = jnp.minimum(jnp.sum((s[:, None] >= cum[None, :]).astype(jnp.int32), axis=1), n_experts - 1)
    live = s < cum[-1]
    pick = g[:, None] == experts[None, :]
    at_g = lambda v: jnp.sum(jnp.where(pick, v[None, :], 0), axis=1)
    tile = jnp.where(live, at_g(first_tile) + s - at_g(cum - ntile), n_tiles - 1)
    lo = jnp.clip(at_g(starts) - tile * MOE_BLOCK, 0, MOE_BLOCK)
    hi = jnp.where(live, jnp.clip(at_g(ends) - tile * MOE_BLOCK, 0, MOE_BLOCK), lo)
    g = jnp.where(live, g, jnp.max(jnp.where(live, g, 0)))
    steps = tuple(v.astype(jnp.int32) for v in (g, tile, lo, hi))
    return gate, pos.reshape(n, TOP_K), tok_sorted, steps


def kernel(x, mem, ln_in_g, ln_in_b, w_in, b_gate, hy_conv_w, hy_conv_b, hy_f1_w, hy_f1_b, hy_f2_w, hy_f2_b, hy_f3_w, hy_freq, hy_skip, sw_sink, mem_w_kv, w_br_hy, w_br_swa, w_br_mem, w_out, ln1_g, ln1_b, router_w, router_b, moe_w1, moe_b1, moe_w2, moe_b2, ln2_g, ln2_b):
    batch, seq, d = x.shape
    depth = w_in.shape[0]
    n = batch * seq
    width = hy_skip.shape[2]
    hy_cols = 3 * width
    q_cols = SW_HEADS * SW_HEAD_DIM
    kv_cols = SW_KV_HEADS * SW_HEAD_DIM
    mq_cols = MEM_HEADS * MEM_HEAD_DIM
    gate_cols = N_BRANCH * d
    mem_tokens = mem.shape[1]
    n_experts = router_w.shape[2]
    alpha = (2 * depth) ** 0.25
    assert batch == 2, "the long convolution packs exactly two batch elements into one complex signal"
    assert w_in.shape[2] == hy_cols + q_cols + 2 * kv_cols + mq_cols + gate_cols
    n1 = 2 * seq // DFT_N2

    cst = _dft_consts(n1, DFT_N2)
    zfeat = _filter_features(seq)
    cos_t, sin_t = _rope_tables(seq)
    swap = _rope_swap_matrix()

    o_q = hy_cols
    o_k = o_q + q_cols
    o_v = o_k + kv_cols
    o_mq = o_v + kv_cols
    o_g = o_mq + mq_cols
    dup = lambda w0: jnp.concatenate(
        [w_in[:, :, w0 + h * SW_HEAD_DIM:w0 + (h + 1) * SW_HEAD_DIM] for h in range(SW_KV_HEADS) for _ in range(2)],
        axis=2)
    w_proj = jnp.concatenate([w_in[:, :, o_g:o_g + gate_cols], w_in[:, :, :hy_cols], w_in[:, :, o_q:o_q + q_cols],
                              w_in[:, :, o_mq:o_mq + mq_cols], dup(o_k), dup(o_v)], axis=2).astype(BF)
    kw = 2 * kv_cols
    c_hy, c_q, c_mq = gate_cols, gate_cols + hy_cols, gate_cols + hy_cols + q_cols
    c_k = c_mq + mq_cols
    c_v = c_k + kw
    proj_cols = c_v + kw
    tn_proj = proj_cols // 4

    de = moe_w2.shape[2]
    b1g = moe_b1[..., 0::2].reshape(depth, n_experts, 1, de)
    b1l = moe_b1[..., 1::2].reshape(depth, n_experts, 1, de)
    b2 = moe_b2.reshape(depth, n_experts, 1, d)
    perm = _deinterleave_matrix()
    mem_bf = mem.reshape(batch * mem_tokens, d).astype(BF)

    filter_stage1 = lambda l, feats: _hyena_filter_stage1(
        seq, width, hy_f1_w[l], hy_f1_b[l], hy_f2_w[l], hy_f2_b[l], hy_f3_w[l], hy_freq[l], feats, cst)

    xf, xb = _ln_call(x.reshape(n, d), ln_in_g, ln_in_b)
    af = filter_stage1(0, zfeat)
    for l in range(depth):
        proj = _mm_call(xb, w_proj[l], BF, 1024, tn_proj, "in_proj")

        hv, hx1, hx2 = _shortconv_call(proj, hy_conv_w[l], hy_conv_b[l], batch, seq, c_hy // hy_cols)
        wide = (2, n1 // 2, DFT_N2 * width)
        y_hy = hv.reshape(wide)
        for o, gx in enumerate((hx1, hx2)):
            y_hy = _hyena_conv(y_hy, gx.reshape(wide), af[o], hy_skip[l, o], cst, seq, width)
        y_hy = y_hy.reshape(n, width)

        y_sw = _attn_call(proj, c_q // q_cols, c_k // kw, q_cols, kw, proj[:, c_v:c_v + kw].T, cos_t, sin_t, swap,
                          sw_sink[l], batch, seq)

        memkv = _mm_call(mem_bf, mem_w_kv[l].astype(BF), BF, mem_tokens, 2 * mq_cols, "mem_kv")
        y_mem = _memattn_call(proj, c_mq // mq_cols, memkv, batch, seq, mem_tokens)

        xf, xb = _merge_call(xf, proj, 0, b_gate[l], y_hy, y_sw, y_mem, w_br_hy[l].astype(BF),
                             w_br_swa[l].astype(BF), w_br_mem[l].astype(BF), w_out[l].astype(BF),
                             ln1_g[l], ln1_b[l], alpha)

        gate, dest, tok_sorted, steps = _route(xf, router_w[l], router_b[l], n_experts)
        if l + 1 < depth:
            anchor = jnp.minimum(tok_sorted[0], 0).astype(F32)
            af = filter_stage1(l + 1, zfeat + anchor)
        xs = xb.at[tok_sorted].get(mode="promise_in_bounds")
        ys = _expert_call(steps, xs, moe_w1, b1g, b1l, moe_w2, b2, perm, l)
        yg = ys.at[dest.T].get(mode="promise_in_bounds")
        xf, xb = _combine_ln_call(xf, yg, gate, ln2_g[l], ln2_b[l], alpha)
    return xf.reshape(batch, seq, d)
```

```python
import functools
import math

import numpy as np
import jax
import jax.numpy as jnp
from jax import lax
from jax.experimental import pallas as pl
from jax.experimental.pallas import tpu as pltpu

BF = jnp.bfloat16
F32 = jnp.float32

HY_ORDER = 2
HY_SHORT = 3
HY_EMB = 33
HY_BANDS = (HY_EMB - 1) // 2
HY_FAST_DECAY = 0.3
HY_SLOW_DECAY = 1.5
HY_DECAY_TARGET = 1e-2
SW_HEADS = 8
SW_KV_HEADS = 2
SW_HEAD_DIM = 64
SW_WINDOW = 128
ROPE_THETA = 500000.0
ROPE_DIM = SW_HEAD_DIM // 4
MEM_HEADS = 4
MEM_HEAD_DIM = 128
N_BRANCH = 3
TOP_K = 4
SWIGLU_ALPHA = 1.702
SWIGLU_LIMIT = 7.0
MOE_BLOCK = 512
LN_EPS = 1e-5
NEG_INF = -1e30

LANES = 128
V7X_VMEM_BYTES = 64 * 1024 * 1024
VMEM_LIMIT = 48 * 1024 * 1024
VMEM_LIMIT_EXPERTS = 58 * 1024 * 1024
DFT_N2 = 128
SUBLANES = 8
FEAT_ROWS = -(-HY_EMB // SUBLANES) * SUBLANES


def _cparams(*sem):
    return pltpu.CompilerParams(dimension_semantics=sem, vmem_limit_bytes=VMEM_LIMIT)


def _ln_rows(r, g, b):
    mu = jnp.mean(r, axis=-1, keepdims=True)
    d = r - mu
    var = jnp.mean(d * d, axis=-1, keepdims=True)
    return d * lax.rsqrt(var + LN_EPS) * g + b


def _ln_kernel(x_ref, g_ref, b_ref, y_ref, yb_ref):
    y = _ln_rows(x_ref[...], g_ref[...], b_ref[...])
    y_ref[...] = y
    yb_ref[...] = y.astype(BF)


def _ln_call(x, g, b, tm=512):
    n, d = x.shape
    row = pl.BlockSpec((tm, d), lambda i: (i, 0))
    vec = pl.BlockSpec((1, d), lambda i: (0, 0))
    return pl.pallas_call(
        _ln_kernel, grid=(n // tm,), in_specs=[row, vec, vec], out_specs=[row, row],
        out_shape=[jax.ShapeDtypeStruct((n, d), F32), jax.ShapeDtypeStruct((n, d), BF)],
        compiler_params=_cparams("parallel"), name="ln_entry",
    )(x, g.reshape(1, d), b.reshape(1, d))


def _combine_ln_kernel(x_ref, y_ref, w_ref, g_ref, b_ref, o_ref, ob_ref, *, alpha):
    w = w_ref[...]
    f = y_ref[0].astype(F32) * w[:, 0:1]
    for k in range(1, TOP_K):
        f = f + y_ref[k].astype(F32) * w[:, k:k + 1]
    y = _ln_rows(alpha * x_ref[...] + f, g_ref[...], b_ref[...])
    o_ref[...] = y
    ob_ref[...] = y.astype(BF)


def _combine_ln_call(x, yg, w, g, b, alpha, tm=512):
    n, d = x.shape
    row = pl.BlockSpec((tm, d), lambda i: (i, 0))
    vec = pl.BlockSpec((1, d), lambda i: (0, 0))
    return pl.pallas_call(
        functools.partial(_combine_ln_kernel, alpha=alpha), grid=(n // tm,),
        in_specs=[row, pl.BlockSpec((TOP_K, tm, d), lambda i: (0, i, 0)),
                  pl.BlockSpec((tm, TOP_K), lambda i: (i, 0)), vec, vec],
        out_specs=[row, row],
        out_shape=[jax.ShapeDtypeStruct((n, d), F32), jax.ShapeDtypeStruct((n, d), BF)],
        compiler_params=_cparams("parallel"), name="moe_combine_ln",
    )(x, yg, w, g.reshape(1, d), b.reshape(1, d))


def _mm_kernel(a_ref, b_ref, o_ref):
    o_ref[...] = jnp.dot(a_ref[...], b_ref[...], preferred_element_type=F32).astype(o_ref.dtype)


def _mm_call(a, b, out_dtype, tm, tn, name):
    m, k = a.shape
    n = b.shape[1]
    return pl.pallas_call(
        _mm_kernel, grid=(n // tn, m // tm),
        in_specs=[pl.BlockSpec((tm, k), lambda j, i: (i, 0)), pl.BlockSpec((k, tn), lambda j, i: (0, j))],
        out_specs=pl.BlockSpec((tm, tn), lambda j, i: (i, j)),
        out_shape=jax.ShapeDtypeStruct((m, n), out_dtype),
        compiler_params=_cparams("parallel", "parallel"), name=name,
    )(a, b)


def _shortconv_kernel(zp_ref, z_ref, zn_ref, w_ref, b_ref, hv_ref, hx1_ref, hx2_ref, *, tl, halo, width):
    i = pl.program_id(1)
    nt = pl.num_programs(1)
    z = z_ref[...].astype(F32)
    prev_row = zp_ref[halo - 1:halo, :].astype(F32)
    next_row = zn_ref[0:1, :].astype(F32)
    prev_row = jnp.where(i == 0, 0.0, prev_row)
    next_row = jnp.where(i == nt - 1, 0.0, next_row)
    row = lax.broadcasted_iota(jnp.int32, z.shape, 0)
    zm1 = jnp.where(row == 0, prev_row, pltpu.roll(z, 1, 0))
    zp1 = jnp.where(row == tl - 1, next_row, pltpu.roll(z, tl - 1, 0))
    w = w_ref[...]
    out = zm1 * w[0:1] + b_ref[...] + z * w[1:2] + zp1 * w[2:3]
    hv_ref[...] = out[:, 0:width].astype(BF)
    hx1_ref[...] = out[:, width:2 * width].astype(BF)
    hx2_ref[...] = out[:, 2 * width:3 * width].astype(BF)


def _shortconv_call(proj, conv_w, conv_b, batch, seq, col_blk, tl=512, halo=16):
    n = proj.shape[0]
    c3 = conv_w.shape[1]
    width = c3 // 3
    nt = seq // tl
    hb = tl // halo
    nhb = seq // halo
    main = pl.BlockSpec((tl, c3), lambda b, i: (b * nt + i, col_blk))
    prev = pl.BlockSpec((halo, c3), lambda b, i: (b * nhb + jnp.maximum(i * hb - 1, 0), col_blk))
    nxt = pl.BlockSpec((halo, c3), lambda b, i: (b * nhb + jnp.minimum((i + 1) * hb, nhb - 1), col_blk))
    out = pl.BlockSpec((tl, width), lambda b, i: (b * nt + i, 0))
    return pl.pallas_call(
        functools.partial(_shortconv_kernel, tl=tl, halo=halo, width=width), grid=(batch, nt),
        in_specs=[prev, main, nxt, pl.BlockSpec((HY_SHORT, c3), lambda b, i: (0, 0)),
                  pl.BlockSpec((1, c3), lambda b, i: (0, 0))],
        out_specs=[out, out, out],
        out_shape=[jax.ShapeDtypeStruct((n, width), BF)] * 3,
        compiler_params=_cparams("parallel", "parallel"), name="hy_shortconv",
    )(proj, proj, proj, conv_w, conv_b.reshape(1, c3))


def _filter_kernel(z_ref, f1w_ref, f1b_ref, f2w_ref, f2b_ref, f3w_ref, freq_ref, adel_ref, k_ref,
                   *, tr, seq, n_inner):
    i = pl.program_id(0)
    hp = lax.Precision.HIGHEST
    fr = freq_ref[...]
    h = jnp.sin(fr * (jnp.dot(f1w_ref[...], z_ref[...], precision=hp, preferred_element_type=F32)
                      + f1b_ref[...]))
    for j in range(n_inner):
        h = jnp.sin(fr * (jnp.dot(f2w_ref[j], h, precision=hp, preferred_element_type=F32) + f2b_ref[j]))
    k = lax.dot_general(h.astype(BF), f3w_ref[...], (((0,), (0,)), ((), ())),
                        preferred_element_type=F32)
    width = adel_ref.shape[1]
    row = i * tr + lax.broadcasted_iota(jnp.int32, (tr, width), 0)
    pos = jnp.where(row < seq, row, 2 * seq - row)
    win = jnp.exp(-(pos.astype(F32) * (1.0 / (seq - 1))) * adel_ref[...])
    win = jnp.where(row == seq, 0.0, win)
    for o in range(HY_ORDER):
        k_ref[o] = (k[:, o * width:(o + 1) * width] * win).astype(k_ref.dtype)


def _filter_call(zfeat_t, f1w_t, f1b, f2w_t, f2b, f3w_dir, freq, adel, seq, tr=1024):
    zr, n2l = zfeat_t.shape
    hid = f1w_t.shape[0]
    n_inner = f2w_t.shape[0]
    ow = f3w_dir.shape[2]
    width = adel.shape[1]
    half = (n2l // tr) // 2
    full = lambda *shape: pl.BlockSpec(shape, lambda i: (0,) * len(shape))
    return pl.pallas_call(
        functools.partial(_filter_kernel, tr=tr, seq=seq, n_inner=n_inner), grid=(n2l // tr,),
        in_specs=[pl.BlockSpec((zr, tr), lambda i: (0, i)), full(hid, zr), full(hid, 1),
                  full(n_inner, hid, hid), full(n_inner, hid, 1),
                  pl.BlockSpec((None, hid, ow), lambda i: (i // half, 0, 0)),
                  full(hid, 1), full(1, width)],
        out_specs=pl.BlockSpec((HY_ORDER, tr, width), lambda i: (0, i, 0)),
        out_shape=jax.ShapeDtypeStruct((HY_ORDER, n2l, width), BF),
        compiler_params=_cparams("parallel"), name="hy_filter",
    )(zfeat_t, f1w_t, f1b, f2w_t, f2b, f3w_dir, freq, adel)


def _dft_consts(n1, n2):
    n = n1 * n2
    h1 = n1 // 2
    k1 = np.arange(n1)[:, None]
    ang = -2.0 * np.pi * (k1 * np.arange(h1)[None, :]) / n1
    cr, ci = np.cos(ang), np.sin(ang)
    e1 = np.block([[cr, -ci], [ci, cr]])
    angf = -2.0 * np.pi * (k1 * np.arange(n1)[None, :]) / n1
    ef = np.concatenate([np.cos(angf), np.sin(angf)], axis=0)
    a2 = -2.0 * np.pi * (np.arange(n2)[:, None] * np.arange(n2)[None, :]) / n2
    f2r, f2i = np.cos(a2), np.sin(a2)
    dm = np.block([[f2r, f2i], [-f2i, f2r]])
    at = -2.0 * np.pi * (np.arange(n1)[:, None] * np.arange(n2)[None, :]) / n
    twr, twi = np.cos(at), np.sin(at)
    ab = 2.0 * np.pi * (np.arange(h1)[:, None] * np.arange(n1)[None, :]) / n1
    br, bi = np.cos(ab) / n, np.sin(ab) / n
    f = lambda a: jnp.asarray(a, dtype=F32)
    return dict(e1=f(e1), ef=f(ef), f2r=f(f2r), f2i=f(f2i), dm=f(dm),
                twr=f(twr.reshape(n1, 1, n2)), twi=f(twi.reshape(n1, 1, n2)),
                ctr=f(twr.T.reshape(n2, 1, n1)), cti=f(-twi.T.reshape(n2, 1, n1)),
                br=f(br), bi=f(bi))


def _dft_s1_kernel(e_ref, top_ref, bot_ref, a_ref):
    rhs = jnp.concatenate([top_ref[...], bot_ref[...]], axis=0).astype(BF)
    res = jnp.dot(e_ref[...], rhs, preferred_element_type=F32)
    n1 = a_ref.shape[1]
    a_ref[0] = res[:n1].astype(a_ref.dtype)
    a_ref[1] = res[n1:].astype(a_ref.dtype)


def _dft_s1_call(e_mat, x3, top_idx, bot_idx, n1, tn=4096):
    h1, cols = x3.shape[1], x3.shape[2]
    return pl.pallas_call(
        _dft_s1_kernel, grid=(cols // tn,),
        in_specs=[pl.BlockSpec((2 * n1, 2 * h1), lambda j: (0, 0)),
                  pl.BlockSpec((None, h1, tn), lambda j: (top_idx, 0, j)),
                  pl.BlockSpec((None, h1, tn), lambda j: (bot_idx, 0, j))],
        out_specs=pl.BlockSpec((2, n1, tn), lambda j: (0, 0, j)),
        out_shape=jax.ShapeDtypeStruct((2, n1, cols), BF),
        compiler_params=_cparams("parallel"), name="hy_dft_stage1",
    )(e_mat.astype(BF), x3, x3)


def _twiddled_f2(f2r, f2i, tr, ti):
    gr = f2r * tr - f2i * ti
    gi = f2r * ti + f2i * tr
    return jnp.concatenate([jnp.concatenate([gr, -gi], axis=1), jnp.concatenate([gi, gr], axis=1)], axis=0)


def _dft_s2_conv_kernel(a_ref, af_ref, f2r_ref, f2i_ref, dm_ref, twr_ref, twi_ref, b_ref, *, kb):
    n2 = f2r_ref.shape[0]
    dm = dm_ref[...]
    for j in range(kb):
        gm = _twiddled_f2(f2r_ref[...], f2i_ref[...], twr_ref[j], twi_ref[j]).astype(BF)
        rhs = jnp.concatenate([a_ref[0, j], a_ref[1, j]], axis=0)
        x = jnp.dot(gm, rhs, preferred_element_type=F32)
        kf = jnp.dot(gm, jnp.concatenate([af_ref[0, j], af_ref[1, j]], axis=0),
                     preferred_element_type=F32)
        xr, xi = x[:n2], x[n2:]
        kr, ki = kf[:n2], kf[n2:]
        y = jnp.concatenate([xr * kr - xi * ki, xr * ki + xi * kr], axis=0).astype(BF)
        bm = jnp.dot(dm, y, preferred_element_type=F32)
        b_ref[0, j] = bm[:n2].astype(b_ref.dtype)
        b_ref[1, j] = bm[n2:].astype(b_ref.dtype)


def _dft_s2_conv_call(a4, af4, cst, kb=16):
    _, n1, n2, c = a4.shape
    blk = pl.BlockSpec((2, kb, n2, c), lambda i: (0, i, 0, 0))
    sq = pl.BlockSpec((n2, n2), lambda i: (0, 0))
    tw = pl.BlockSpec((kb, 1, n2), lambda i: (i, 0, 0))
    return pl.pallas_call(
        functools.partial(_dft_s2_conv_kernel, kb=kb), grid=(n1 // kb,),
        in_specs=[blk, blk, sq, sq, pl.BlockSpec((2 * n2, 2 * n2), lambda i: (0, 0)), tw, tw],
        out_specs=blk, out_shape=jax.ShapeDtypeStruct(a4.shape, BF),
        compiler_params=_cparams("parallel"), name="hy_dft_stage2_conv",
    )(a4, af4, cst["f2r"], cst["f2i"], cst["dm"].astype(BF), cst["twr"], cst["twi"])


def _dft_s1inv_kernel(b_ref, br_ref, bi_ref, ctr_ref, cti_ref, u_ref, gx_ref, skip_ref, o_ref, *, ng, c):
    h1 = br_ref.shape[0]
    skip = skip_ref[...]
    for j in range(ng):
        tr, ti = ctr_ref[j], cti_ref[j]
        mr = br_ref[...] * tr - bi_ref[...] * ti
        mi = br_ref[...] * ti + bi_ref[...] * tr
        em = jnp.concatenate([jnp.concatenate([mr, -mi], axis=1),
                              jnp.concatenate([mi, mr], axis=1)], axis=0).astype(BF)
        y = jnp.dot(em, b_ref[:, j * c:(j + 1) * c], preferred_element_type=F32)
        for b in range(2):
            u = u_ref[b, :, j * c:(j + 1) * c].astype(F32)
            g = gx_ref[b, :, j * c:(j + 1) * c].astype(F32)
            o_ref[b, :, j * c:(j + 1) * c] = (g * (y[b * h1:(b + 1) * h1] + skip * u)).astype(o_ref.dtype)


def _dft_s1inv_call(b2, u3, gx3, skip, cst, c, ng=4):
    rows, cols = b2.shape
    h1 = u3.shape[1]
    n2 = cols // c
    n1 = rows // 2
    ub = pl.BlockSpec((2, h1, ng * c), lambda i: (0, 0, i))
    cb = pl.BlockSpec((h1, n1), lambda i: (0, 0))
    tw = pl.BlockSpec((ng, 1, n1), lambda i: (i, 0, 0))
    return pl.pallas_call(
        functools.partial(_dft_s1inv_kernel, ng=ng, c=c), grid=(n2 // ng,),
        in_specs=[pl.BlockSpec((rows, ng * c), lambda i: (0, i)), cb, cb, tw, tw, ub, ub,
                  pl.BlockSpec((1, c), lambda i: (0, 0))],
        out_specs=ub, out_shape=jax.ShapeDtypeStruct(u3.shape, BF),
        compiler_params=_cparams("parallel"), name="hy_dft_stage1_inverse",
    )(b2, cst["br"], cst["bi"], cst["ctr"], cst["cti"], u3, gx3, skip.reshape(1, c))


def _filter_features(seq):
    t01 = jnp.linspace(0.0, 1.0, seq, dtype=F32)[:, None]
    w = (2.0 * math.pi) * jnp.arange(seq, dtype=F32)[:, None] / seq
    bands = jnp.linspace(1e-4, HY_BANDS - 1, HY_BANDS, dtype=F32)
    z = jnp.concatenate([t01, jnp.cos(bands * w), -jnp.sin(bands * w)], axis=-1)
    z2 = jnp.concatenate([z, z[:1], z[1:][::-1]], axis=0)
    return jnp.pad(z2, ((0, 0), (0, FEAT_ROWS - HY_EMB))).T


def _hyena_filter_stage1(seq, width, f1_w, f1_b, f2_w, f2_b, f3_w, freq, zfeat_t, cst):
    hid = f1_w.shape[1]
    n1 = 2 * seq // DFT_N2
    max_decay = math.log(HY_DECAY_TARGET) / HY_FAST_DECAY
    min_decay = math.log(HY_DECAY_TARGET) / HY_SLOW_DECAY
    adel = jnp.abs(jnp.linspace(min_decay, max_decay, width, dtype=F32)).reshape(1, width)
    f1w_t = jnp.pad(f1_w, ((0, FEAT_ROWS - HY_EMB), (0, 0))).T
    f3d = f3_w.reshape(hid, HY_ORDER, 2, width).transpose(2, 0, 1, 3).reshape(2, hid, HY_ORDER * width)
    k3 = _filter_call(zfeat_t, f1w_t, f1_b.reshape(hid, 1), f2_w.transpose(0, 2, 1), f2_b.reshape(-1, hid, 1),
                      f3d.astype(BF), freq.reshape(hid, 1), adel, seq)
    kw = k3.reshape(HY_ORDER * 2, n1 // 2, DFT_N2 * width)
    return [_dft_s1_call(cst["ef"], kw, 2 * o, 2 * o + 1, n1).reshape(2, n1, DFT_N2, width)
            for o in range(HY_ORDER)]


def _hyena_conv(u3, gx3, af, skip, cst, seq, width):
    n1 = 2 * seq // DFT_N2
    a = _dft_s1_call(cst["e1"], u3, 0, 1, n1)
    bm = _dft_s2_conv_call(a.reshape(2, n1, DFT_N2, width), af, cst)
    return _dft_s1inv_call(bm.reshape(2 * n1, DFT_N2 * width), u3, gx3, skip, cst, width)


def _rope_tables(seq):
    half = ROPE_DIM // 2
    inv = jnp.power(jnp.float32(ROPE_THETA), -jnp.arange(half, dtype=F32) * (2.0 / ROPE_DIM))
    ang = jnp.arange(seq, dtype=jnp.int32).astype(F32)[:, None] * inv
    cos, sin = jnp.cos(ang), jnp.sin(ang)
    rest = SW_HEAD_DIM - ROPE_DIM
    ones = jnp.ones((seq, rest), F32)
    zeros = jnp.zeros((seq, rest), F32)
    ch = jnp.concatenate([cos, cos, ones], axis=1)
    sh = jnp.concatenate([-sin, sin, zeros], axis=1)
    reps = LANES // SW_HEAD_DIM
    return jnp.tile(ch, (1, reps)), jnp.tile(sh, (1, reps))


def _rope_swap_matrix():
    half = ROPE_DIM // 2
    p = np.zeros((LANES, LANES), np.float32)
    for j in range(LANES):
        d = j % SW_HEAD_DIM
        if d < half:
            p[j + half, j] = 1.0
        elif d < ROPE_DIM:
            p[j - half, j] = 1.0
    return jnp.asarray(p)


def _attn_kernel(sink_ref, q_ref, kp_ref, km_ref, kn_ref, vp_ref, vm_ref, vn_ref, cp_ref, cm_ref, cn_ref,
                 sp_ref, sm_ref, sn_ref, swap_ref, o_ref, *, tq, seq):
    blk = SW_WINDOW
    i = pl.program_id(1)
    swap = swap_ref[...]

    def rot(x, c, s):
        return x.astype(F32) * c + jnp.dot(x, swap, preferred_element_type=F32) * s

    cwin = jnp.concatenate([cp_ref[...], cm_ref[...], cn_ref[...]], axis=0)
    swin = jnp.concatenate([sp_ref[...], sm_ref[...], sn_ref[...]], axis=0)
    kraw = jnp.concatenate([kp_ref[...], km_ref[...], kn_ref[...]], axis=0)
    kwin = jnp.concatenate([rot(kraw[:, h * LANES:(h + 1) * LANES], cwin, swin).astype(BF)
                            for h in range(SW_KV_HEADS)], axis=1)
    q_scale = SW_HEAD_DIM ** -0.5
    qrot = [(rot(q_ref[:, g * LANES:(g + 1) * LANES], cm_ref[...], sm_ref[...]) * q_scale).astype(BF)
            for g in range(q_ref.shape[1] // LANES)]
    vwin_t = jnp.concatenate([vp_ref[...], vm_ref[...], vn_ref[...]], axis=1)
    lo = lax.broadcasted_iota(jnp.int32, (blk, LANES), 1) < SW_HEAD_DIM
    zero = jnp.zeros((), BF)
    per_group = LANES // SW_HEAD_DIM
    groups_per_kv = (SW_HEADS // SW_KV_HEADS) // per_group
    stack = groups_per_kv * per_group
    w_idx = lax.broadcasted_iota(jnp.int32, (3 * blk, stack * blk), 0)
    a_idx = lax.broadcasted_iota(jnp.int32, (3 * blk, stack * blk), 1) & (blk - 1)
    rel = w_idx - a_idx
    for jb in range(tq // blk):
        kpos = i * tq + (jb - 1) * blk + w_idx
        bad = (rel < 0) | (rel > 2 * SW_WINDOW) | (kpos < 0) | (kpos >= seq)
        for h in range(SW_KV_HEADS):
            parts, sinks = [], []
            for gg in range(groups_per_kv):
                g = h * groups_per_kv + gg
                qg = qrot[g][jb * blk:(jb + 1) * blk]
                parts += [jnp.where(lo, qg, zero), jnp.where(lo, zero, qg)]
                sinks += [jnp.full((1, blk), sink_ref[g * per_group + par], F32) for par in range(per_group)]
            qs = jnp.concatenate(parts, axis=0)
            sk = jnp.concatenate(sinks, axis=1)
            kh = kwin[jb * blk:(jb + 3) * blk, h * LANES:(h + 1) * LANES]
            vt = vwin_t[h * LANES:(h + 1) * LANES, jb * blk:(jb + 3) * blk]
            s = lax.dot_general(kh, qs, (((1,), (1,)), ((), ())), preferred_element_type=F32)
            s = jnp.where(bad, NEG_INF, s)
            m = jnp.maximum(jnp.max(s, axis=0, keepdims=True), sk)
            p = jnp.exp(s - m)
            denom = jnp.sum(p, axis=0, keepdims=True) + jnp.exp(sk - m)
            ot = jnp.dot(vt, p.astype(BF), preferred_element_type=F32) / denom
            o = ot.T
            for gg in range(groups_per_kv):
                g = h * groups_per_kv + gg
                oa = o[(gg * per_group) * blk:(gg * per_group + 1) * blk]
                ob = o[(gg * per_group + 1) * blk:(gg * per_group + 2) * blk]
                o_ref[jb * blk:(jb + 1) * blk, g * LANES:(g + 1) * LANES] = jnp.where(lo, oa, ob).astype(BF)


def _attn_call(proj, q_blk, k_blk, qw, kw, v_t, cos_t, sin_t, swap, sink, batch, seq, tq=512):
    n = proj.shape[0]
    blk = SW_WINDOW
    nt = seq // tq
    r = tq // blk
    nb = seq // blk
    prev_p = lambda i: jnp.maximum(i * r - 1, 0)
    next_p = lambda i: jnp.minimum((i + 1) * r, nb - 1)
    tab = lambda rows, f: pl.BlockSpec((rows, LANES), lambda b, i, s: (f(i), 0))
    tabs = [tab(blk, prev_p), tab(tq, lambda i: i), tab(blk, next_p)]
    grid_spec = pltpu.PrefetchScalarGridSpec(
        num_scalar_prefetch=1, grid=(batch, nt),
        in_specs=[pl.BlockSpec((tq, qw), lambda b, i, s: (b * nt + i, q_blk)),
                  pl.BlockSpec((blk, kw), lambda b, i, s: (b * nb + prev_p(i), k_blk)),
                  pl.BlockSpec((tq, kw), lambda b, i, s: (b * nt + i, k_blk)),
                  pl.BlockSpec((blk, kw), lambda b, i, s: (b * nb + next_p(i), k_blk)),
                  pl.BlockSpec((kw, blk), lambda b, i, s: (0, b * nb + prev_p(i))),
                  pl.BlockSpec((kw, tq), lambda b, i, s: (0, b * nt + i)),
                  pl.BlockSpec((kw, blk), lambda b, i, s: (0, b * nb + next_p(i)))]
                 + tabs + tabs + [pl.BlockSpec((LANES, LANES), lambda b, i, s: (0, 0))],
        out_specs=pl.BlockSpec((tq, qw), lambda b, i, s: (b * nt + i, 0)))
    return pl.pallas_call(
        functools.partial(_attn_kernel, tq=tq, seq=seq), grid_spec=grid_spec,
        out_shape=jax.ShapeDtypeStruct((n, qw), BF),
        compiler_params=_cparams("parallel", "parallel"), name="sw_attention",
    )(sink, proj, proj, proj, proj, v_t, v_t, v_t, cos_t, cos_t, cos_t, sin_t, sin_t, sin_t, swap.astype(BF))


def _memattn_kernel(q_ref, mk_ref, mv_ref, o_ref):
    scale = MEM_HEAD_DIM ** -0.5
    for h in range(MEM_HEADS):
        sl = slice(h * MEM_HEAD_DIM, (h + 1) * MEM_HEAD_DIM)
        s = lax.dot_general(q_ref[:, sl], mk_ref[:, sl], (((1,), (1,)), ((), ())),
                            preferred_element_type=F32) * scale
        m = jnp.max(s, axis=-1, keepdims=True)
        p = jnp.exp(s - m)
        denom = jnp.sum(p, axis=-1, keepdims=True)
        o = jnp.dot(p.astype(BF), mv_ref[:, sl], preferred_element_type=F32) / denom
        o_ref[:, sl] = o.astype(BF)


def _memattn_call(proj, q_blk, memkv, batch, seq, mem_tokens, tm=512):
    n = proj.shape[0]
    w = MEM_HEADS * MEM_HEAD_DIM
    nt = seq // tm
    return pl.pallas_call(
        _memattn_kernel, grid=(batch, nt),
        in_specs=[pl.BlockSpec((tm, w), lambda b, i: (b * nt + i, q_blk)),
                  pl.BlockSpec((mem_tokens, w), lambda b, i: (b, 0)),
                  pl.BlockSpec((mem_tokens, w), lambda b, i: (b, 1))],
        out_specs=pl.BlockSpec((tm, w), lambda b, i: (b * nt + i, 0)),
        out_shape=jax.ShapeDtypeStruct((n, w), BF),
        compiler_params=_cparams("parallel", "parallel"), name="mem_attention",
    )(proj, memkv, memkv)


def _merge_kernel(x_ref, gl_ref, bg_ref, yh_ref, ys_ref, ym_ref, wh_ref, ws_ref, wm_ref, wo_ref,
                  g_ref, b_ref, o_ref, ob_ref, *, alpha, d):
    merged = None
    for br, (y_ref, w_ref) in enumerate(((yh_ref, wh_ref), (ys_ref, ws_ref), (ym_ref, wm_ref))):
        gate = jax.nn.sigmoid(gl_ref[:, br * d:(br + 1) * d].astype(F32) + bg_ref[br:br + 1, :])
        t = gate * jnp.dot(y_ref[...], w_ref[...], preferred_element_type=F32)
        merged = t if merged is None else merged + t
    h = jnp.dot(merged.astype(BF), wo_ref[...], preferred_element_type=F32)
    y = _ln_rows(alpha * x_ref[...] + h, g_ref[...], b_ref[...])
    o_ref[...] = y
    ob_ref[...] = y.astype(BF)


def _merge_call(x, proj, gate_blk, b_gate, y_hy, y_sw, y_mem, w_hy, w_sw, w_mem, w_out, g, b, alpha, tm=512):
    n, d = x.shape
    row = lambda w: pl.BlockSpec((tm, w), lambda i: (i, 0))
    full = lambda a: pl.BlockSpec(a.shape, lambda i: (0,) * a.ndim)
    g2, b2 = g.reshape(1, d), b.reshape(1, d)
    return pl.pallas_call(
        functools.partial(_merge_kernel, alpha=alpha, d=d), grid=(n // tm,),
        in_specs=[row(d), pl.BlockSpec((tm, N_BRANCH * d), lambda i: (i, gate_blk)), full(b_gate),
                  row(y_hy.shape[1]), row(y_sw.shape[1]), row(y_mem.shape[1]),
                  full(w_hy), full(w_sw), full(w_mem), full(w_out), full(g2), full(b2)],
        out_specs=[row(d), row(d)],
        out_shape=[jax.ShapeDtypeStruct((n, d), F32), jax.ShapeDtypeStruct((n, d), BF)],
        compiler_params=_cparams("parallel"), name="merge_outproj_ln",
    )(x, proj, b_gate, y_hy, y_sw, y_mem, w_hy, w_sw, w_mem, w_out, g2, b2)


def _deinterleave_matrix():
    perm = np.zeros((2 * LANES, 2 * LANES), np.float32)
    perm[2 * np.arange(LANES), np.arange(LANES)] = 1.0
    perm[2 * np.arange(LANES) + 1, LANES + np.arange(LANES)] = 1.0
    return jnp.asarray(perm).astype(BF)


def _expert_kernel(sg_ref, st_ref, lo_ref, hi_ref, x_ref, w1_ref, b1g_ref, b1l_ref, w2_ref, b2_ref, perm_ref,
                   y_ref, w1g_ref, w1l_ref, w2b_ref):
    s = pl.program_id(0)
    lo, hi = lo_ref[s], hi_ref[s]
    prev = jnp.maximum(s - 1, 0)

    @pl.when((s == 0) | (sg_ref[s] != sg_ref[prev]))
    def _():
        p = perm_ref[...]
        for c in range(w1g_ref.shape[1] // LANES):
            r = jnp.dot(w1_ref[:, 2 * c * LANES:2 * (c + 1) * LANES].astype(BF), p, preferred_element_type=F32)
            w1g_ref[:, c * LANES:(c + 1) * LANES] = r[:, :LANES].astype(BF)
            w1l_ref[:, c * LANES:(c + 1) * LANES] = r[:, LANES:].astype(BF)
        w2b_ref[...] = w2_ref[...].astype(BF)

    @pl.when(hi > lo)
    def _():
        x = x_ref[...]
        hg = jnp.dot(x, w1g_ref[...], preferred_element_type=F32) + b1g_ref[...]
        hl = jnp.dot(x, w1l_ref[...], preferred_element_type=F32) + b1l_ref[...]
        hg = jnp.minimum(hg, SWIGLU_LIMIT)
        hl = jnp.clip(hl, -SWIGLU_LIMIT, SWIGLU_LIMIT)
        act = hg * jax.nn.sigmoid(SWIGLU_ALPHA * hg) * (hl + 1.0)
        y = jnp.dot(act.astype(BF), w2b_ref[...], preferred_element_type=F32) + b2_ref[...]
        whole = (lo == 0) & (hi == y.shape[0])
        first_visit = (s == 0) | (st_ref[s] != st_ref[prev])

        @pl.when(whole)
        def _():
            y_ref[...] = y.astype(y_ref.dtype)

        def masked(prev):
            row = lax.broadcasted_iota(jnp.int32, y.shape, 0)
            y_ref[...] = jnp.where((row >= lo) & (row < hi), y, prev).astype(y_ref.dtype)

        @pl.when(jnp.logical_not(whole) & first_visit)
        def _():
            masked(jnp.zeros_like(y))

        @pl.when(jnp.logical_not(whole) & jnp.logical_not(first_visit))
        def _():
            masked(y_ref[...].astype(F32))


def _expert_call(steps, xs, w1, b1g, b1l, w2, b2, perm, layer):
    a, d = xs.shape
    de = w2.shape[2]
    n_steps = steps[0].shape[0]
    wspec = lambda r, c: pl.BlockSpec((None, None, r, c), lambda i, sg, st, lo, hi: (layer, sg[i], 0, 0))
    rows = pl.BlockSpec((MOE_BLOCK, d), lambda i, sg, st, lo, hi: (st[i], 0))
    grid_spec = pltpu.PrefetchScalarGridSpec(
        num_scalar_prefetch=4, grid=(n_steps,),
        in_specs=[rows, wspec(d, 2 * de), wspec(1, de), wspec(1, de), wspec(de, d), wspec(1, d),
                  pl.BlockSpec(perm.shape, lambda i, sg, st, lo, hi: (0, 0))],
        out_specs=rows,
        scratch_shapes=[pltpu.VMEM((d, de), BF), pltpu.VMEM((d, de), BF), pltpu.VMEM((de, d), BF)])
    return pl.pallas_call(
        _expert_kernel, grid_spec=grid_spec, out_shape=jax.ShapeDtypeStruct((a, d), BF),
        compiler_params=pltpu.CompilerParams(dimension_semantics=("arbitrary",),
                                             vmem_limit_bytes=VMEM_LIMIT_EXPERTS), name="moe_experts",
    )(*steps, xs, w1, b1g, b1l, w2, b2, perm)


def _router_kernel(x_ref, wh_ref, wl_ref, b_ref, tri_ref, ir_ref, gate_ref, cnt_ref, carry_ref):
    @pl.when(pl.program_id(0) == 0)
    def _():
        carry_ref[...] = jnp.zeros_like(carry_ref)

    x = x_ref[...]
    xh = x.astype(BF)
    xl = (x - xh.astype(F32)).astype(BF)
    wh = wh_ref[...]
    logits = (jnp.dot(xh, wh, preferred_element_type=F32) + jnp.dot(xl, wh, preferred_element_type=F32)
              + jnp.dot(xh, wl_ref[...], preferred_element_type=F32)) + b_ref[...]
    lane_i = lax.broadcasted_iota(jnp.int32, logits.shape, 1)
    lane = lane_i.astype(F32)
    cur = logits
    vals, hots, idxs = [], [], []
    for k in range(TOP_K):
        m = jnp.max(cur, axis=-1, keepdims=True)
        idx = jnp.min(jnp.where(cur == m, lane, float(LANES)), axis=-1, keepdims=True)
        hot = lane == idx
        cur = jnp.where(hot, NEG_INF, cur)
        vals.append(m)
        hots.append(hot.astype(F32))
        idxs.append(idx)
    ex = [jnp.exp(v - vals[0]) for v in vals]
    tot = ex[0]
    for e in ex[1:]:
        tot = tot + e
    for k in range(TOP_K):
        gate_ref[:, k:k + 1] = ex[k] / tot
    chosen = hots[0]
    for h in hots[1:]:
        chosen = chosen + h
    before = jnp.dot(tri_ref[...], chosen.astype(BF), preferred_element_type=F32) + carry_ref[...]
    cols = jnp.zeros(logits.shape, F32)
    for k in range(TOP_K):
        cols = jnp.where(lane_i == k, idxs[k], cols)
        cols = jnp.where(lane_i == TOP_K + k, jnp.sum(hots[k] * before, axis=-1, keepdims=True), cols)
    ir_ref[...] = cols.T[:2 * TOP_K].astype(jnp.int32)
    carry_ref[...] = carry_ref[...] + jnp.sum(chosen, axis=0, keepdims=True)
    cnt_ref[...] = carry_ref[...].astype(jnp.int32)


def _router_call(x, router_w, router_b, tm=512):
    n, d = x.shape
    ne = router_w.shape[1]
    assert ne <= LANES
    w = jnp.pad(router_w, ((0, 0), (0, LANES - ne)))
    wh = w.astype(BF)
    wl = (w - wh.astype(F32)).astype(BF)
    b = jnp.pad(router_b, (0, LANES - ne), constant_values=NEG_INF).reshape(1, LANES)
    tri = jnp.asarray(np.tril(np.ones((tm, tm), np.float32), -1)).astype(BF)
    wspec = pl.BlockSpec((d, LANES), lambda i: (0, 0))
    return pl.pallas_call(
        _router_kernel, grid=(n // tm,),
        in_specs=[pl.BlockSpec((tm, d), lambda i: (i, 0)), wspec, wspec,
                  pl.BlockSpec((1, LANES), lambda i: (0, 0)), pl.BlockSpec((tm, tm), lambda i: (0, 0))],
        out_specs=[pl.BlockSpec((2 * TOP_K, tm), lambda i: (0, i)), pl.BlockSpec((tm, TOP_K), lambda i: (i, 0)),
                   pl.BlockSpec((1, LANES), lambda i: (0, 0))],
        out_shape=[jax.ShapeDtypeStruct((2 * TOP_K, n), jnp.int32), jax.ShapeDtypeStruct((n, TOP_K), F32),
                   jax.ShapeDtypeStruct((1, LANES), jnp.int32)],
        scratch_shapes=[pltpu.VMEM((1, LANES), F32)],
        compiler_params=_cparams("arbitrary"), name="moe_router",
    )(x, wh, wl, b, tri)


def _route(x, router_w, router_b, n_experts):
    n = x.shape[0]
    ir, gate, cnt = _router_call(x, router_w, router_b)
    top_i, rank = ir[:TOP_K], ir[TOP_K:]
    a = n * TOP_K
    assert a % MOE_BLOCK == 0 and n_experts * a < 2 ** 31
    tok = jnp.arange(n, dtype=jnp.int32)[None, :]
    slot = jnp.arange(TOP_K, dtype=jnp.int32)[:, None]
    skey = jnp.sort((top_i * a + tok * TOP_K + slot).reshape(-1))
    tok_sorted = (skey % a) // TOP_K
    experts = jnp.arange(n_experts, dtype=jnp.int32)
    counts = cnt[0, :n_experts]
    ends = jnp.cumsum(counts)
    starts = ends - counts
    pos = rank
    for e in range(n_experts):
        pos = pos + jnp.where(top_i == e, starts[e], 0)
    n_tiles = a // MOE_BLOCK
    first_tile = starts // MOE_BLOCK
    ntile = jnp.where(counts > 0, (ends - 1) // MOE_BLOCK - first_tile + 1, 0)
    cum = jnp.cumsum(ntile)
    n_steps = n_tiles + n_experts - 1
    s = jnp.arange(n_steps, dtype=jnp.int32)
    g = jnp.minimum(jnp.sum((s[:, None] >= cum[None, :]).astype(jnp.int32), axis=1), n_experts - 1)
    live = s < cum[-1]
    pick = g[:, None] == experts[None, :]
    at_g = lambda v: jnp.sum(jnp.where(pick, v[None, :], 0), axis=1)
    tile = jnp.where(live, at_g(first_tile) + s - at_g(cum - ntile), n_tiles - 1)
    lo = jnp.clip(at_g(starts) - tile * MOE_BLOCK, 0, MOE_BLOCK)
    hi = jnp.where(live, jnp.clip(at_g(ends) - tile * MOE_BLOCK, 0, MOE_BLOCK), lo)
    g = jnp.where(live, g, jnp.max(jnp.where(live, g, 0)))
    steps = tuple(v.astype(jnp.int32) for v in (g, tile, lo, hi))
    return gate, pos, tok_sorted, steps


def kernel(x, mem, ln_in_g, ln_in_b, w_in, b_gate, hy_conv_w, hy_conv_b, hy_f1_w, hy_f1_b, hy_f2_w, hy_f2_b, hy_f3_w, hy_freq, hy_skip, sw_sink, mem_w_kv, w_br_hy, w_br_swa, w_br_mem, w_out, ln1_g, ln1_b, router_w, router_b, moe_w1, moe_b1, moe_w2, moe_b2, ln2_g, ln2_b):
    batch, seq, d = x.shape
    depth = w_in.shape[0]
    n = batch * seq
    width = hy_skip.shape[2]
    hy_cols = 3 * width
    q_cols = SW_HEADS * SW_HEAD_DIM
    kv_cols = SW_KV_HEADS * SW_HEAD_DIM
    mq_cols = MEM_HEADS * MEM_HEAD_DIM
    gate_cols = N_BRANCH * d
    mem_tokens = mem.shape[1]
    n_experts = router_w.shape[2]
    alpha = (2 * depth) ** 0.25
    assert batch == 2, "the long convolution packs exactly two batch elements into one complex signal"
    assert w_in.shape[2] == hy_cols + q_cols + 2 * kv_cols + mq_cols + gate_cols
    n1 = 2 * seq // DFT_N2

    cst = _dft_consts(n1, DFT_N2)
    zfeat = _filter_features(seq)
    cos_t, sin_t = _rope_tables(seq)
    swap = _rope_swap_matrix()

    o_q = hy_cols
    o_k = o_q + q_cols
    o_v = o_k + kv_cols
    o_mq = o_v + kv_cols
    o_g = o_mq + mq_cols
    dup = lambda w0: jnp.concatenate(
        [w_in[:, :, w0 + h * SW_HEAD_DIM:w0 + (h + 1) * SW_HEAD_DIM] for h in range(SW_KV_HEADS) for _ in range(2)],
        axis=2)
    w_proj = jnp.concatenate([w_in[:, :, o_g:o_g + gate_cols], w_in[:, :, :hy_cols], w_in[:, :, o_q:o_q + q_cols],
                              w_in[:, :, o_mq:o_mq + mq_cols], dup(o_k), dup(o_v)], axis=2).astype(BF)
    kw = 2 * kv_cols
    c_hy, c_q, c_mq = gate_cols, gate_cols + hy_cols, gate_cols + hy_cols + q_cols
    c_k = c_mq + mq_cols
    c_v = c_k + kw
    proj_cols = c_v + kw
    tn_proj = proj_cols // 4

    de = moe_w2.shape[2]
    b1g = moe_b1[..., 0::2].reshape(depth, n_experts, 1, de)
    b1l = moe_b1[..., 1::2].reshape(depth, n_experts, 1, de)
    b2 = moe_b2.reshape(depth, n_experts, 1, d)
    perm = _deinterleave_matrix()
    mem_bf = mem.reshape(batch * mem_tokens, d).astype(BF)

    filter_stage1 = lambda l, feats: _hyena_filter_stage1(
        seq, width, hy_f1_w[l], hy_f1_b[l], hy_f2_w[l], hy_f2_b[l], hy_f3_w[l], hy_freq[l], feats, cst)

    xf, xb = _ln_call(x.reshape(n, d), ln_in_g, ln_in_b)
    af = filter_stage1(0, zfeat)
    for l in range(depth):
        proj = _mm_call(xb, w_proj[l], BF, 1024, tn_proj, "in_proj")

        hv, hx1, hx2 = _shortconv_call(proj, hy_conv_w[l], hy_conv_b[l], batch, seq, c_hy // hy_cols)
        wide = (2, n1 // 2, DFT_N2 * width)
        y_hy = hv.reshape(wide)
        for o, gx in enumerate((hx1, hx2)):
            y_hy = _hyena_conv(y_hy, gx.reshape(wide), af[o], hy_skip[l, o], cst, seq, width)
        y_hy = y_hy.reshape(n, width)

        y_sw = _attn_call(proj, c_q // q_cols, c_k // kw, q_cols, kw, proj[:, c_v:c_v + kw].T, cos_t, sin_t, swap,
                          sw_sink[l], batch, seq)

        memkv = _mm_call(mem_bf, mem_w_kv[l].astype(BF), BF, mem_tokens, 2 * mq_cols, "mem_kv")
        y_mem = _memattn_call(proj, c_mq // mq_cols, memkv, batch, seq, mem_tokens)

        xf, xb = _merge_call(xf, proj, 0, b_gate[l], y_hy, y_sw, y_mem, w_br_hy[l].astype(BF),
                             w_br_swa[l].astype(BF), w_br_mem[l].astype(BF), w_out[l].astype(BF),
                             ln1_g[l], ln1_b[l], alpha)

        gate, dest, tok_sorted, steps = _route(xf, router_w[l], router_b[l], n_experts)
        if l + 1 < depth:
            anchor = jnp.minimum(tok_sorted[0], 0).astype(F32)
            af = filter_stage1(l + 1, zfeat + anchor)
        xs = xb.at[tok_sorted].get(mode="promise_in_bounds")
        ys = _expert_call(steps, xs, moe_w1, b1g, b1l, moe_w2, b2, perm, l)
        yg = ys.at[dest].get(mode="promise_in_bounds")
        xf, xb = _combine_ln_call(xf, yg, gate, ln2_g[l], ln2_b[l], alpha)
    return xf.reshape(batch, seq, d)
```

```python
import functools
import math

import numpy as np
import jax
import jax.numpy as jnp
from jax import lax
from jax.experimental import pallas as pl
from jax.experimental.pallas import tpu as pltpu

BF = jnp.bfloat16
F32 = jnp.float32

HY_ORDER = 2
HY_SHORT = 3
HY_EMB = 33
HY_BANDS = (HY_EMB - 1) // 2
HY_FAST_DECAY = 0.3
HY_SLOW_DECAY = 1.5
HY_DECAY_TARGET = 1e-2
SW_HEADS = 8
SW_KV_HEADS = 2
SW_HEAD_DIM = 64
SW_WINDOW = 128
ROPE_THETA = 500000.0
ROPE_DIM = SW_HEAD_DIM // 4
MEM_HEADS = 4
MEM_HEAD_DIM = 128
N_BRANCH = 3
TOP_K = 4
SWIGLU_ALPHA = 1.702
SWIGLU_LIMIT = 7.0
MOE_BLOCK = 512
LN_EPS = 1e-5
NEG_INF = -1e30

LANES = 128
V7X_VMEM_BYTES = 64 * 1024 * 1024
VMEM_LIMIT = 48 * 1024 * 1024
VMEM_LIMIT_EXPERTS = 58 * 1024 * 1024
DFT_N2 = 128
SUBLANES = 8
FEAT_ROWS = -(-HY_EMB // SUBLANES) * SUBLANES


def _cparams(*sem):
    return pltpu.CompilerParams(dimension_semantics=sem, vmem_limit_bytes=VMEM_LIMIT)


def _ln_rows(r, g, b):
    mu = jnp.mean(r, axis=-1, keepdims=True)
    d = r - mu
    var = jnp.mean(d * d, axis=-1, keepdims=True)
    return d * lax.rsqrt(var + LN_EPS) * g + b


def _ln_kernel(x_ref, g_ref, b_ref, y_ref, yb_ref):
    y = _ln_rows(x_ref[...], g_ref[...], b_ref[...])
    y_ref[...] = y
    yb_ref[...] = y.astype(BF)


def _ln_call(x, g, b, tm=512):
    n, d = x.shape
    row = pl.BlockSpec((tm, d), lambda i: (i, 0))
    vec = pl.BlockSpec((1, d), lambda i: (0, 0))
    return pl.pallas_call(
        _ln_kernel, grid=(n // tm,), in_specs=[row, vec, vec], out_specs=[row, row],
        out_shape=[jax.ShapeDtypeStruct((n, d), F32), jax.ShapeDtypeStruct((n, d), BF)],
        compiler_params=_cparams("parallel"), name="ln_entry",
    )(x, g.reshape(1, d), b.reshape(1, d))


def _combine_ln_kernel(x_ref, y_ref, w_ref, g_ref, b_ref, o_ref, ob_ref, *, alpha):
    w = w_ref[...]
    f = y_ref[0].astype(F32) * w[:, 0:1]
    for k in range(1, TOP_K):
        f = f + y_ref[k].astype(F32) * w[:, k:k + 1]
    y = _ln_rows(alpha * x_ref[...] + f, g_ref[...], b_ref[...])
    o_ref[...] = y
    ob_ref[...] = y.astype(BF)


def _combine_ln_call(x, yg, w, g, b, alpha, tm=512):
    n, d = x.shape
    row = pl.BlockSpec((tm, d), lambda i: (i, 0))
    vec = pl.BlockSpec((1, d), lambda i: (0, 0))
    return pl.pallas_call(
        functools.partial(_combine_ln_kernel, alpha=alpha), grid=(n // tm,),
        in_specs=[row, pl.BlockSpec((TOP_K, tm, d), lambda i: (0, i, 0)),
                  pl.BlockSpec((tm, TOP_K), lambda i: (i, 0)), vec, vec],
        out_specs=[row, row],
        out_shape=[jax.ShapeDtypeStruct((n, d), F32), jax.ShapeDtypeStruct((n, d), BF)],
        compiler_params=_cparams("parallel"), name="moe_combine_ln",
    )(x, yg, w, g.reshape(1, d), b.reshape(1, d))


def _mm_kernel(a_ref, b_ref, o_ref):
    o_ref[...] = jnp.dot(a_ref[...], b_ref[...], preferred_element_type=F32).astype(o_ref.dtype)


def _mm_call(a, b, out_dtype, tm, tn, name):
    m, k = a.shape
    n = b.shape[1]
    return pl.pallas_call(
        _mm_kernel, grid=(n // tn, m // tm),
        in_specs=[pl.BlockSpec((tm, k), lambda j, i: (i, 0)), pl.BlockSpec((k, tn), lambda j, i: (0, j))],
        out_specs=pl.BlockSpec((tm, tn), lambda j, i: (i, j)),
        out_shape=jax.ShapeDtypeStruct((m, n), out_dtype),
        compiler_params=_cparams("parallel", "parallel"), name=name,
    )(a, b)


def _shortconv_kernel(zp_ref, z_ref, zn_ref, w_ref, b_ref, hv_ref, hx1_ref, hx2_ref, *, tl, halo, width):
    i = pl.program_id(1)
    nt = pl.num_programs(1)
    z = z_ref[...].astype(F32)
    prev_row = zp_ref[halo - 1:halo, :].astype(F32)
    next_row = zn_ref[0:1, :].astype(F32)
    prev_row = jnp.where(i == 0, 0.0, prev_row)
    next_row = jnp.where(i == nt - 1, 0.0, next_row)
    row = lax.broadcasted_iota(jnp.int32, z.shape, 0)
    zm1 = jnp.where(row == 0, prev_row, pltpu.roll(z, 1, 0))
    zp1 = jnp.where(row == tl - 1, next_row, pltpu.roll(z, tl - 1, 0))
    w = w_ref[...]
    out = zm1 * w[0:1] + b_ref[...] + z * w[1:2] + zp1 * w[2:3]
    hv_ref[...] = out[:, 0:width].astype(BF)
    hx1_ref[...] = out[:, width:2 * width].astype(BF)
    hx2_ref[...] = out[:, 2 * width:3 * width].astype(BF)


def _shortconv_call(proj, conv_w, conv_b, batch, seq, col_blk, tl=512, halo=16):
    n = proj.shape[0]
    c3 = conv_w.shape[1]
    width = c3 // 3
    nt = seq // tl
    hb = tl // halo
    nhb = seq // halo
    main = pl.BlockSpec((tl, c3), lambda b, i: (b * nt + i, col_blk))
    prev = pl.BlockSpec((halo, c3), lambda b, i: (b * nhb + jnp.maximum(i * hb - 1, 0), col_blk))
    nxt = pl.BlockSpec((halo, c3), lambda b, i: (b * nhb + jnp.minimum((i + 1) * hb, nhb - 1), col_blk))
    out = pl.BlockSpec((tl, width), lambda b, i: (b * nt + i, 0))
    return pl.pallas_call(
        functools.partial(_shortconv_kernel, tl=tl, halo=halo, width=width), grid=(batch, nt),
        in_specs=[prev, main, nxt, pl.BlockSpec((HY_SHORT, c3), lambda b, i: (0, 0)),
                  pl.BlockSpec((1, c3), lambda b, i: (0, 0))],
        out_specs=[out, out, out],
        out_shape=[jax.ShapeDtypeStruct((n, width), BF)] * 3,
        compiler_params=_cparams("parallel", "parallel"), name="hy_shortconv",
    )(proj, proj, proj, conv_w, conv_b.reshape(1, c3))


def _filter_kernel(z_ref, f1w_ref, f1b_ref, f2w_ref, f2b_ref, f3w_ref, freq_ref, adel_ref, k_ref,
                   *, tr, seq, n_inner):
    i = pl.program_id(0)
    hp = lax.Precision.HIGHEST
    fr = freq_ref[...]
    h = jnp.sin(fr * (jnp.dot(f1w_ref[...], z_ref[...], precision=hp, preferred_element_type=F32)
                      + f1b_ref[...]))
    for j in range(n_inner):
        h = jnp.sin(fr * (jnp.dot(f2w_ref[j], h, precision=hp, preferred_element_type=F32) + f2b_ref[j]))
    k = lax.dot_general(h.astype(BF), f3w_ref[...], (((0,), (0,)), ((), ())),
                        preferred_element_type=F32)
    width = adel_ref.shape[1]
    row = i * tr + lax.broadcasted_iota(jnp.int32, (tr, width), 0)
    pos = jnp.where(row < seq, row, 2 * seq - row)
    win = jnp.exp(-(pos.astype(F32) * (1.0 / (seq - 1))) * adel_ref[...])
    win = jnp.where(row == seq, 0.0, win)
    for o in range(HY_ORDER):
        k_ref[o] = (k[:, o * width:(o + 1) * width] * win).astype(k_ref.dtype)


def _filter_call(zfeat_t, f1w_t, f1b, f2w_t, f2b, f3w_dir, freq, adel, seq, tr=1024):
    zr, n2l = zfeat_t.shape
    hid = f1w_t.shape[0]
    n_inner = f2w_t.shape[0]
    ow = f3w_dir.shape[2]
    width = adel.shape[1]
    half = (n2l // tr) // 2
    full = lambda *shape: pl.BlockSpec(shape, lambda i: (0,) * len(shape))
    return pl.pallas_call(
        functools.partial(_filter_kernel, tr=tr, seq=seq, n_inner=n_inner), grid=(n2l // tr,),
        in_specs=[pl.BlockSpec((zr, tr), lambda i: (0, i)), full(hid, zr), full(hid, 1),
                  full(n_inner, hid, hid), full(n_inner, hid, 1),
                  pl.BlockSpec((None, hid, ow), lambda i: (i // half, 0, 0)),
                  full(hid, 1), full(1, width)],
        out_specs=pl.BlockSpec((HY_ORDER, tr, width), lambda i: (0, i, 0)),
        out_shape=jax.ShapeDtypeStruct((HY_ORDER, n2l, width), BF),
        compiler_params=_cparams("parallel"), name="hy_filter",
    )(zfeat_t, f1w_t, f1b, f2w_t, f2b, f3w_dir, freq, adel)


def _dft_consts(n1, n2):
    n = n1 * n2
    h1 = n1 // 2
    k1 = np.arange(n1)[:, None]
    ang = -2.0 * np.pi * (k1 * np.arange(h1)[None, :]) / n1
    cr, ci = np.cos(ang), np.sin(ang)
    e1 = np.block([[cr, -ci], [ci, cr]])
    angf = -2.0 * np.pi * (k1 * np.arange(n1)[None, :]) / n1
    ef = np.concatenate([np.cos(angf), np.sin(angf)], axis=0)
    a2 = -2.0 * np.pi * (np.arange(n2)[:, None] * np.arange(n2)[None, :]) / n2
    f2r, f2i = np.cos(a2), np.sin(a2)
    dm = np.block([[f2r, f2i], [-f2i, f2r]])
    at = -2.0 * np.pi * (np.arange(n1)[:, None] * np.arange(n2)[None, :]) / n
    twr, twi = np.cos(at), np.sin(at)
    ab = 2.0 * np.pi * (np.arange(h1)[:, None] * np.arange(n1)[None, :]) / n1
    br, bi = np.cos(ab) / n, np.sin(ab) / n
    f = lambda a: jnp.asarray(a, dtype=F32)
    return dict(e1=f(e1), ef=f(ef), f2r=f(f2r), f2i=f(f2i), dm=f(dm),
                twr=f(twr.reshape(n1, 1, n2)), twi=f(twi.reshape(n1, 1, n2)),
                ctr=f(twr.T.reshape(n2, 1, n1)), cti=f(-twi.T.reshape(n2, 1, n1)),
                br=f(br), bi=f(bi))


def _dft_s1_kernel(e_ref, top_ref, bot_ref, a_ref):
    rhs = jnp.concatenate([top_ref[...], bot_ref[...]], axis=0).astype(BF)
    res = jnp.dot(e_ref[...], rhs, preferred_element_type=F32)
    n1 = a_ref.shape[1]
    a_ref[0] = res[:n1].astype(a_ref.dtype)
    a_ref[1] = res[n1:].astype(a_ref.dtype)


def _dft_s1_call(e_mat, x3, top_idx, bot_idx, n1, tn=4096):
    h1, cols = x3.shape[1], x3.shape[2]
    return pl.pallas_call(
        _dft_s1_kernel, grid=(cols // tn,),
        in_specs=[pl.BlockSpec((2 * n1, 2 * h1), lambda j: (0, 0)),
                  pl.BlockSpec((None, h1, tn), lambda j: (top_idx, 0, j)),
                  pl.BlockSpec((None, h1, tn), lambda j: (bot_idx, 0, j))],
        out_specs=pl.BlockSpec((2, n1, tn), lambda j: (0, 0, j)),
        out_shape=jax.ShapeDtypeStruct((2, n1, cols), BF),
        compiler_params=_cparams("parallel"), name="hy_dft_stage1",
    )(e_mat.astype(BF), x3, x3)


def _twiddled_f2(f2r, f2i, tr, ti):
    gr = f2r * tr - f2i * ti
    gi = f2r * ti + f2i * tr
    return jnp.concatenate([jnp.concatenate([gr, -gi], axis=1), jnp.concatenate([gi, gr], axis=1)], axis=0)


def _dft_s2_conv_kernel(a_ref, af_ref, f2r_ref, f2i_ref, dm_ref, twr_ref, twi_ref, b_ref, *, kb):
    n2 = f2r_ref.shape[0]
    dm = dm_ref[...]
    for j in range(kb):
        gm = _twiddled_f2(f2r_ref[...], f2i_ref[...], twr_ref[j], twi_ref[j]).astype(BF)
        rhs = jnp.concatenate([a_ref[0, j], a_ref[1, j]], axis=0)
        x = jnp.dot(gm, rhs, preferred_element_type=F32)
        kf = jnp.dot(gm, jnp.concatenate([af_ref[0, j], af_ref[1, j]], axis=0),
                     preferred_element_type=F32)
        xr, xi = x[:n2], x[n2:]
        kr, ki = kf[:n2], kf[n2:]
        y = jnp.concatenate([xr * kr - xi * ki, xr * ki + xi * kr], axis=0).astype(BF)
        bm = jnp.dot(dm, y, preferred_element_type=F32)
        b_ref[0, j] = bm[:n2].astype(b_ref.dtype)
        b_ref[1, j] = bm[n2:].astype(b_ref.dtype)


def _dft_s2_conv_call(a4, af4, cst, kb=16):
    _, n1, n2, c = a4.shape
    blk = pl.BlockSpec((2, kb, n2, c), lambda i: (0, i, 0, 0))
    sq = pl.BlockSpec((n2, n2), lambda i: (0, 0))
    tw = pl.BlockSpec((kb, 1, n2), lambda i: (i, 0, 0))
    return pl.pallas_call(
        functools.partial(_dft_s2_conv_kernel, kb=kb), grid=(n1 // kb,),
        in_specs=[blk, blk, sq, sq, pl.BlockSpec((2 * n2, 2 * n2), lambda i: (0, 0)), tw, tw],
        out_specs=blk, out_shape=jax.ShapeDtypeStruct(a4.shape, BF),
        compiler_params=_cparams("parallel"), name="hy_dft_stage2_conv",
    )(a4, af4, cst["f2r"], cst["f2i"], cst["dm"].astype(BF), cst["twr"], cst["twi"])


def _dft_s1inv_kernel(b_ref, br_ref, bi_ref, ctr_ref, cti_ref, u_ref, gx_ref, skip_ref, o_ref, *, ng, c):
    h1 = br_ref.shape[0]
    skip = skip_ref[...]
    for j in range(ng):
        tr, ti = ctr_ref[j], cti_ref[j]
        mr = br_ref[...] * tr - bi_ref[...] * ti
        mi = br_ref[...] * ti + bi_ref[...] * tr
        em = jnp.concatenate([jnp.concatenate([mr, -mi], axis=1),
                              jnp.concatenate([mi, mr], axis=1)], axis=0).astype(BF)
        y = jnp.dot(em, b_ref[:, j * c:(j + 1) * c], preferred_element_type=F32)
        for b in range(2):
            u = u_ref[b, :, j * c:(j + 1) * c].astype(F32)
            g = gx_ref[b, :, j * c:(j + 1) * c].astype(F32)
            o_ref[b, :, j * c:(j + 1) * c] = (g * (y[b * h1:(b + 1) * h1] + skip * u)).astype(o_ref.dtype)


def _dft_s1inv_call(b2, u3, gx3, skip, cst, c, ng=4):
    rows, cols = b2.shape
    h1 = u3.shape[1]
    n2 = cols // c
    n1 = rows // 2
    ub = pl.BlockSpec((2, h1, ng * c), lambda i: (0, 0, i))
    cb = pl.BlockSpec((h1, n1), lambda i: (0, 0))
    tw = pl.BlockSpec((ng, 1, n1), lambda i: (i, 0, 0))
    return pl.pallas_call(
        functools.partial(_dft_s1inv_kernel, ng=ng, c=c), grid=(n2 // ng,),
        in_specs=[pl.BlockSpec((rows, ng * c), lambda i: (0, i)), cb, cb, tw, tw, ub, ub,
                  pl.BlockSpec((1, c), lambda i: (0, 0))],
        out_specs=ub, out_shape=jax.ShapeDtypeStruct(u3.shape, BF),
        compiler_params=_cparams("parallel"), name="hy_dft_stage1_inverse",
    )(b2, cst["br"], cst["bi"], cst["ctr"], cst["cti"], u3, gx3, skip.reshape(1, c))


def _filter_features(seq):
    t01 = jnp.linspace(0.0, 1.0, seq, dtype=F32)[:, None]
    w = (2.0 * math.pi) * jnp.arange(seq, dtype=F32)[:, None] / seq
    bands = jnp.linspace(1e-4, HY_BANDS - 1, HY_BANDS, dtype=F32)
    z = jnp.concatenate([t01, jnp.cos(bands * w), -jnp.sin(bands * w)], axis=-1)
    z2 = jnp.concatenate([z, z[:1], z[1:][::-1]], axis=0)
    return jnp.pad(z2, ((0, 0), (0, FEAT_ROWS - HY_EMB))).T


def _hyena_filter_taps(seq, width, f1_w, f1_b, f2_w, f2_b, f3_w, freq, zfeat_t):
    hid = f1_w.shape[1]
    n1 = 2 * seq // DFT_N2
    max_decay = math.log(HY_DECAY_TARGET) / HY_FAST_DECAY
    min_decay = math.log(HY_DECAY_TARGET) / HY_SLOW_DECAY
    adel = jnp.abs(jnp.linspace(min_decay, max_decay, width, dtype=F32)).reshape(1, width)
    f1w_t = jnp.pad(f1_w, ((0, FEAT_ROWS - HY_EMB), (0, 0))).T
    f3d = f3_w.reshape(hid, HY_ORDER, 2, width).transpose(2, 0, 1, 3).reshape(2, hid, HY_ORDER * width)
    k3 = _filter_call(zfeat_t, f1w_t, f1_b.reshape(hid, 1), f2_w.transpose(0, 2, 1), f2_b.reshape(-1, hid, 1),
                      f3d.astype(BF), freq.reshape(hid, 1), adel, seq)
    return k3.reshape(HY_ORDER * 2, n1 // 2, DFT_N2 * width)


def _hyena_filter_stage1(taps, seq, width, cst):
    n1 = 2 * seq // DFT_N2
    return [_dft_s1_call(cst["ef"], taps, 2 * o, 2 * o + 1, n1).reshape(2, n1, DFT_N2, width)
            for o in range(HY_ORDER)]


def _hyena_conv(u3, gx3, af, skip, cst, seq, width):
    n1 = 2 * seq // DFT_N2
    a = _dft_s1_call(cst["e1"], u3, 0, 1, n1)
    bm = _dft_s2_conv_call(a.reshape(2, n1, DFT_N2, width), af, cst)
    return _dft_s1inv_call(bm.reshape(2 * n1, DFT_N2 * width), u3, gx3, skip, cst, width)


def _rope_tables(seq):
    half = ROPE_DIM // 2
    inv = jnp.power(jnp.float32(ROPE_THETA), -jnp.arange(half, dtype=F32) * (2.0 / ROPE_DIM))
    ang = jnp.arange(seq, dtype=jnp.int32).astype(F32)[:, None] * inv
    cos, sin = jnp.cos(ang), jnp.sin(ang)
    rest = SW_HEAD_DIM - ROPE_DIM
    ones = jnp.ones((seq, rest), F32)
    zeros = jnp.zeros((seq, rest), F32)
    ch = jnp.concatenate([cos, cos, ones], axis=1)
    sh = jnp.concatenate([-sin, sin, zeros], axis=1)
    reps = LANES // SW_HEAD_DIM
    return jnp.tile(ch, (1, reps)), jnp.tile(sh, (1, reps))


def _rope_swap_matrix():
    half = ROPE_DIM // 2
    p = np.zeros((LANES, LANES), np.float32)
    for j in range(LANES):
        d = j % SW_HEAD_DIM
        if d < half:
            p[j + half, j] = 1.0
        elif d < ROPE_DIM:
            p[j - half, j] = 1.0
    return jnp.asarray(p)


def _attn_kernel(sink_ref, q_ref, kp_ref, km_ref, kn_ref, vp_ref, vm_ref, vn_ref, cp_ref, cm_ref, cn_ref,
                 sp_ref, sm_ref, sn_ref, swap_ref, o_ref, *, tq, seq):
    blk = SW_WINDOW
    i = pl.program_id(1)
    swap = swap_ref[...]

    def rot(x, c, s):
        return x.astype(F32) * c + jnp.dot(x, swap, preferred_element_type=F32) * s

    cwin = jnp.concatenate([cp_ref[...], cm_ref[...], cn_ref[...]], axis=0)
    swin = jnp.concatenate([sp_ref[...], sm_ref[...], sn_ref[...]], axis=0)
    kraw = jnp.concatenate([kp_ref[...], km_ref[...], kn_ref[...]], axis=0)
    kwin = jnp.concatenate([rot(kraw[:, h * LANES:(h + 1) * LANES], cwin, swin).astype(BF)
                            for h in range(SW_KV_HEADS)], axis=1)
    q_scale = SW_HEAD_DIM ** -0.5
    qrot = [(rot(q_ref[:, g * LANES:(g + 1) * LANES], cm_ref[...], sm_ref[...]) * q_scale).astype(BF)
            for g in range(q_ref.shape[1] // LANES)]
    vwin_t = jnp.concatenate([vp_ref[...], vm_ref[...], vn_ref[...]], axis=1)
    lo = lax.broadcasted_iota(jnp.int32, (blk, LANES), 1) < SW_HEAD_DIM
    zero = jnp.zeros((), BF)
    per_group = LANES // SW_HEAD_DIM
    groups_per_kv = (SW_HEADS // SW_KV_HEADS) // per_group
    stack = groups_per_kv * per_group
    w_idx = lax.broadcasted_iota(jnp.int32, (3 * blk, stack * blk), 0)
    a_idx = lax.broadcasted_iota(jnp.int32, (3 * blk, stack * blk), 1) & (blk - 1)
    rel = w_idx - a_idx
    for jb in range(tq // blk):
        kpos = i * tq + (jb - 1) * blk + w_idx
        bad = (rel < 0) | (rel > 2 * SW_WINDOW) | (kpos < 0) | (kpos >= seq)
        for h in range(SW_KV_HEADS):
            parts, sinks = [], []
            for gg in range(groups_per_kv):
                g = h * groups_per_kv + gg
                qg = qrot[g][jb * blk:(jb + 1) * blk]
                parts += [jnp.where(lo, qg, zero), jnp.where(lo, zero, qg)]
                sinks += [jnp.full((1, blk), sink_ref[g * per_group + par], F32) for par in range(per_group)]
            qs = jnp.concatenate(parts, axis=0)
            sk = jnp.concatenate(sinks, axis=1)
            kh = kwin[jb * blk:(jb + 3) * blk, h * LANES:(h + 1) * LANES]
            vt = vwin_t[h * LANES:(h + 1) * LANES, jb * blk:(jb + 3) * blk]
            s = lax.dot_general(kh, qs, (((1,), (1,)), ((), ())), preferred_element_type=F32)
            s = jnp.where(bad, NEG_INF, s)
            m = jnp.maximum(jnp.max(s, axis=0, keepdims=True), sk)
            p = jnp.exp(s - m)
            denom = jnp.sum(p, axis=0, keepdims=True) + jnp.exp(sk - m)
            ot = jnp.dot(vt, p.astype(BF), preferred_element_type=F32) / denom
            o = ot.T
            for gg in range(groups_per_kv):
                g = h * groups_per_kv + gg
                oa = o[(gg * per_group) * blk:(gg * per_group + 1) * blk]
                ob = o[(gg * per_group + 1) * blk:(gg * per_group + 2) * blk]
                o_ref[jb * blk:(jb + 1) * blk, g * LANES:(g + 1) * LANES] = jnp.where(lo, oa, ob).astype(BF)


def _attn_call(proj, q_blk, k_blk, qw, kw, v_t, cos_t, sin_t, swap, sink, batch, seq, tq=512):
    n = proj.shape[0]
    blk = SW_WINDOW
    nt = seq // tq
    r = tq // blk
    nb = seq // blk
    prev_p = lambda i: jnp.maximum(i * r - 1, 0)
    next_p = lambda i: jnp.minimum((i + 1) * r, nb - 1)
    tab = lambda rows, f: pl.BlockSpec((rows, LANES), lambda b, i, s: (f(i), 0))
    tabs = [tab(blk, prev_p), tab(tq, lambda i: i), tab(blk, next_p)]
    grid_spec = pltpu.PrefetchScalarGridSpec(
        num_scalar_prefetch=1, grid=(batch, nt),
        in_specs=[pl.BlockSpec((tq, qw), lambda b, i, s: (b * nt + i, q_blk)),
                  pl.BlockSpec((blk, kw), lambda b, i, s: (b * nb + prev_p(i), k_blk)),
                  pl.BlockSpec((tq, kw), lambda b, i, s: (b * nt + i, k_blk)),
                  pl.BlockSpec((blk, kw), lambda b, i, s: (b * nb + next_p(i), k_blk)),
                  pl.BlockSpec((kw, blk), lambda b, i, s: (0, b * nb + prev_p(i))),
                  pl.BlockSpec((kw, tq), lambda b, i, s: (0, b * nt + i)),
                  pl.BlockSpec((kw, blk), lambda b, i, s: (0, b * nb + next_p(i)))]
                 + tabs + tabs + [pl.BlockSpec((LANES, LANES), lambda b, i, s: (0, 0))],
        out_specs=pl.BlockSpec((tq, qw), lambda b, i, s: (b * nt + i, 0)))
    return pl.pallas_call(
        functools.partial(_attn_kernel, tq=tq, seq=seq), grid_spec=grid_spec,
        out_shape=jax.ShapeDtypeStruct((n, qw), BF),
        compiler_params=_cparams("parallel", "parallel"), name="sw_attention",
    )(sink, proj, proj, proj, proj, v_t, v_t, v_t, cos_t, cos_t, cos_t, sin_t, sin_t, sin_t, swap.astype(BF))


def _memattn_kernel(q_ref, mk_ref, mv_ref, o_ref):
    scale = MEM_HEAD_DIM ** -0.5
    for h in range(MEM_HEADS):
        sl = slice(h * MEM_HEAD_DIM, (h + 1) * MEM_HEAD_DIM)
        s = lax.dot_general(q_ref[:, sl], mk_ref[:, sl], (((1,), (1,)), ((), ())),
                            preferred_element_type=F32) * scale
        m = jnp.max(s, axis=-1, keepdims=True)
        p = jnp.exp(s - m)
        denom = jnp.sum(p, axis=-1, keepdims=True)
        o = jnp.dot(p.astype(BF), mv_ref[:, sl], preferred_element_type=F32) / denom
        o_ref[:, sl] = o.astype(BF)


def _memattn_call(proj, q_blk, memkv, batch, seq, mem_tokens, tm=512):
    n = proj.shape[0]
    w = MEM_HEADS * MEM_HEAD_DIM
    nt = seq // tm
    return pl.pallas_call(
        _memattn_kernel, grid=(batch, nt),
        in_specs=[pl.BlockSpec((tm, w), lambda b, i: (b * nt + i, q_blk)),
                  pl.BlockSpec((mem_tokens, w), lambda b, i: (b, 0)),
                  pl.BlockSpec((mem_tokens, w), lambda b, i: (b, 1))],
        out_specs=pl.BlockSpec((tm, w), lambda b, i: (b * nt + i, 0)),
        out_shape=jax.ShapeDtypeStruct((n, w), BF),
        compiler_params=_cparams("parallel", "parallel"), name="mem_attention",
    )(proj, memkv, memkv)


def _merge_kernel(x_ref, gl_ref, bg_ref, yh_ref, ys_ref, ym_ref, wh_ref, ws_ref, wm_ref, wo_ref,
                  g_ref, b_ref, o_ref, ob_ref, *, alpha, d):
    merged = None
    for br, (y_ref, w_ref) in enumerate(((yh_ref, wh_ref), (ys_ref, ws_ref), (ym_ref, wm_ref))):
        gate = jax.nn.sigmoid(gl_ref[:, br * d:(br + 1) * d].astype(F32) + bg_ref[br:br + 1, :])
        t = gate * jnp.dot(y_ref[...], w_ref[...], preferred_element_type=F32)
        merged = t if merged is None else merged + t
    h = jnp.dot(merged.astype(BF), wo_ref[...], preferred_element_type=F32)
    y = _ln_rows(alpha * x_ref[...] + h, g_ref[...], b_ref[...])
    o_ref[...] = y
    ob_ref[...] = y.astype(BF)


def _merge_call(x, proj, gate_blk, b_gate, y_hy, y_sw, y_mem, w_hy, w_sw, w_mem, w_out, g, b, alpha, tm=512):
    n, d = x.shape
    row = lambda w: pl.BlockSpec((tm, w), lambda i: (i, 0))
    full = lambda a: pl.BlockSpec(a.shape, lambda i: (0,) * a.ndim)
    g2, b2 = g.reshape(1, d), b.reshape(1, d)
    return pl.pallas_call(
        functools.partial(_merge_kernel, alpha=alpha, d=d), grid=(n // tm,),
        in_specs=[row(d), pl.BlockSpec((tm, N_BRANCH * d), lambda i: (i, gate_blk)), full(b_gate),
                  row(y_hy.shape[1]), row(y_sw.shape[1]), row(y_mem.shape[1]),
                  full(w_hy), full(w_sw), full(w_mem), full(w_out), full(g2), full(b2)],
        out_specs=[row(d), row(d)],
        out_shape=[jax.ShapeDtypeStruct((n, d), F32), jax.ShapeDtypeStruct((n, d), BF)],
        compiler_params=_cparams("parallel"), name="merge_outproj_ln",
    )(x, proj, b_gate, y_hy, y_sw, y_mem, w_hy, w_sw, w_mem, w_out, g2, b2)


def _deinterleave_matrix():
    perm = np.zeros((2 * LANES, 2 * LANES), np.float32)
    perm[2 * np.arange(LANES), np.arange(LANES)] = 1.0
    perm[2 * np.arange(LANES) + 1, LANES + np.arange(LANES)] = 1.0
    return jnp.asarray(perm).astype(BF)


def _expert_kernel(sg_ref, st_ref, lo_ref, hi_ref, x_ref, w1_ref, b1g_ref, b1l_ref, w2_ref, b2_ref, perm_ref,
                   y_ref, w1g_ref, w1l_ref, w2b_ref):
    s = pl.program_id(0)
    lo, hi = lo_ref[s], hi_ref[s]
    prev = jnp.maximum(s - 1, 0)

    @pl.when((s == 0) | (sg_ref[s] != sg_ref[prev]))
    def _():
        p = perm_ref[...]
        for c in range(w1g_ref.shape[1] // LANES):
            r = jnp.dot(w1_ref[:, 2 * c * LANES:2 * (c + 1) * LANES].astype(BF), p, preferred_element_type=F32)
            w1g_ref[:, c * LANES:(c + 1) * LANES] = r[:, :LANES].astype(BF)
            w1l_ref[:, c * LANES:(c + 1) * LANES] = r[:, LANES:].astype(BF)
        w2b_ref[...] = w2_ref[...].astype(BF)

    @pl.when(hi > lo)
    def _():
        x = x_ref[...]
        hg = jnp.dot(x, w1g_ref[...], preferred_element_type=F32) + b1g_ref[...]
        hl = jnp.dot(x, w1l_ref[...], preferred_element_type=F32) + b1l_ref[...]
        hg = jnp.minimum(hg, SWIGLU_LIMIT)
        hl = jnp.clip(hl, -SWIGLU_LIMIT, SWIGLU_LIMIT)
        act = hg * jax.nn.sigmoid(SWIGLU_ALPHA * hg) * (hl + 1.0)
        y = jnp.dot(act.astype(BF), w2b_ref[...], preferred_element_type=F32) + b2_ref[...]
        whole = (lo == 0) & (hi == y.shape[0])
        first_visit = (s == 0) | (st_ref[s] != st_ref[prev])

        @pl.when(whole)
        def _():
            y_ref[...] = y.astype(y_ref.dtype)

        def masked(prev):
            row = lax.broadcasted_iota(jnp.int32, y.shape, 0)
            y_ref[...] = jnp.where((row >= lo) & (row < hi), y, prev).astype(y_ref.dtype)

        @pl.when(jnp.logical_not(whole) & first_visit)
        def _():
            masked(jnp.zeros_like(y))

        @pl.when(jnp.logical_not(whole) & jnp.logical_not(first_visit))
        def _():
            masked(y_ref[...].astype(F32))


def _expert_call(steps, xs, w1, b1g, b1l, w2, b2, perm, layer):
    a, d = xs.shape
    de = w2.shape[2]
    n_steps = steps[0].shape[0]
    wspec = lambda r, c: pl.BlockSpec((None, None, r, c), lambda i, sg, st, lo, hi: (layer, sg[i], 0, 0))
    rows = pl.BlockSpec((MOE_BLOCK, d), lambda i, sg, st, lo, hi: (st[i], 0))
    grid_spec = pltpu.PrefetchScalarGridSpec(
        num_scalar_prefetch=4, grid=(n_steps,),
        in_specs=[rows, wspec(d, 2 * de), wspec(1, de), wspec(1, de), wspec(de, d), wspec(1, d),
                  pl.BlockSpec(perm.shape, lambda i, sg, st, lo, hi: (0, 0))],
        out_specs=rows,
        scratch_shapes=[pltpu.VMEM((d, de), BF), pltpu.VMEM((d, de), BF), pltpu.VMEM((de, d), BF)])
    return pl.pallas_call(
        _expert_kernel, grid_spec=grid_spec, out_shape=jax.ShapeDtypeStruct((a, d), BF),
        compiler_params=pltpu.CompilerParams(dimension_semantics=("arbitrary",),
                                             vmem_limit_bytes=VMEM_LIMIT_EXPERTS), name="moe_experts",
    )(*steps, xs, w1, b1g, b1l, w2, b2, perm)


def _router_kernel(x_ref, wh_ref, wl_ref, b_ref, tri_ref, ir_ref, gate_ref, cnt_ref, carry_ref):
    @pl.when(pl.program_id(0) == 0)
    def _():
        carry_ref[...] = jnp.zeros_like(carry_ref)

    x = x_ref[...]
    xh = x.astype(BF)
    xl = (x - xh.astype(F32)).astype(BF)
    wh = wh_ref[...]
    logits = (jnp.dot(xh, wh, preferred_element_type=F32) + jnp.dot(xl, wh, preferred_element_type=F32)
              + jnp.dot(xh, wl_ref[...], preferred_element_type=F32)) + b_ref[...]
    lane_i = lax.broadcasted_iota(jnp.int32, logits.shape, 1)
    lane = lane_i.astype(F32)
    cur = logits
    vals, hots, idxs = [], [], []
    for k in range(TOP_K):
        m = jnp.max(cur, axis=-1, keepdims=True)
        idx = jnp.min(jnp.where(cur == m, lane, float(LANES)), axis=-1, keepdims=True)
        hot = lane == idx
        cur = jnp.where(hot, NEG_INF, cur)
        vals.append(m)
        hots.append(hot.astype(F32))
        idxs.append(idx)
    ex = [jnp.exp(v - vals[0]) for v in vals]
    tot = ex[0]
    for e in ex[1:]:
        tot = tot + e
    for k in range(TOP_K):
        gate_ref[:, k:k + 1] = ex[k] / tot
    chosen = hots[0]
    for h in hots[1:]:
        chosen = chosen + h
    before = jnp.dot(tri_ref[...], chosen.astype(BF), preferred_element_type=F32) + carry_ref[...]
    cols = jnp.zeros(logits.shape, F32)
    for k in range(TOP_K):
        cols = jnp.where(lane_i == k, idxs[k], cols)
        cols = jnp.where(lane_i == TOP_K + k, jnp.sum(hots[k] * before, axis=-1, keepdims=True), cols)
    ir_ref[...] = cols.T[:2 * TOP_K].astype(jnp.int32)
    carry_ref[...] = carry_ref[...] + jnp.sum(chosen, axis=0, keepdims=True)
    cnt_ref[...] = carry_ref[...].astype(jnp.int32)


def _router_call(x, router_w, router_b, tm=512):
    n, d = x.shape
    ne = router_w.shape[1]
    assert ne <= LANES
    w = jnp.pad(router_w, ((0, 0), (0, LANES - ne)))
    wh = w.astype(BF)
    wl = (w - wh.astype(F32)).astype(BF)
    b = jnp.pad(router_b, (0, LANES - ne), constant_values=NEG_INF).reshape(1, LANES)
    tri = jnp.asarray(np.tril(np.ones((tm, tm), np.float32), -1)).astype(BF)
    wspec = pl.BlockSpec((d, LANES), lambda i: (0, 0))
    return pl.pallas_call(
        _router_kernel, grid=(n // tm,),
        in_specs=[pl.BlockSpec((tm, d), lambda i: (i, 0)), wspec, wspec,
                  pl.BlockSpec((1, LANES), lambda i: (0, 0)), pl.BlockSpec((tm, tm), lambda i: (0, 0))],
        out_specs=[pl.BlockSpec((2 * TOP_K, tm), lambda i: (0, i)), pl.BlockSpec((tm, TOP_K), lambda i: (i, 0)),
                   pl.BlockSpec((1, LANES), lambda i: (0, 0))],
        out_shape=[jax.ShapeDtypeStruct((2 * TOP_K, n), jnp.int32), jax.ShapeDtypeStruct((n, TOP_K), F32),
                   jax.ShapeDtypeStruct((1, LANES), jnp.int32)],
        scratch_shapes=[pltpu.VMEM((1, LANES), F32)],
        compiler_params=_cparams("arbitrary"), name="moe_router",
    )(x, wh, wl, b, tri)


def _route(x, router_w, router_b, n_experts):
    n = x.shape[0]
    ir, gate, cnt = _router_call(x, router_w, router_b)
    top_i, rank = ir[:TOP_K], ir[TOP_K:]
    a = n * TOP_K
    assert a % MOE_BLOCK == 0 and n_experts * a < 2 ** 31
    tok = jnp.arange(n, dtype=jnp.int32)[None, :]
    slot = jnp.arange(TOP_K, dtype=jnp.int32)[:, None]
    skey = jnp.sort((top_i * a + tok * TOP_K + slot).reshape(-1))
    tok_sorted = (skey % a) // TOP_K
    experts = jnp.arange(n_experts, dtype=jnp.int32)
    counts = cnt[0, :n_experts]
    ends = jnp.cumsum(counts)
    starts = ends - counts
    pos = rank
    for e in range(n_experts):
        pos = pos + jnp.where(top_i == e, starts[e], 0)
    n_tiles = a // MOE_BLOCK
    first_tile = starts // MOE_BLOCK
    ntile = jnp.where(counts > 0, (ends - 1) // MOE_BLOCK - first_tile + 1, 0)
    cum = jnp.cumsum(ntile)
    n_steps = n_tiles + n_experts - 1
    s = jnp.arange(n_steps, dtype=jnp.int32)
    g = jnp.minimum(jnp.sum((s[:, None] >= cum[None, :]).astype(jnp.int32), axis=1), n_experts - 1)
    live = s < cum[-1]
    pick = g[:, None] == experts[None, :]
    at_g = lambda v: jnp.sum(jnp.where(pick, v[None, :], 0), axis=1)
    tile = jnp.where(live, at_g(first_tile) + s - at_g(cum - ntile), n_tiles - 1)
    lo = jnp.clip(at_g(starts) - tile * MOE_BLOCK, 0, MOE_BLOCK)
    hi = jnp.where(live, jnp.clip(at_g(ends) - tile * MOE_BLOCK, 0, MOE_BLOCK), lo)
    g = jnp.where(live, g, jnp.max(jnp.where(live, g, 0)))
    steps = tuple(v.astype(jnp.int32) for v in (g, tile, lo, hi))
    return gate, pos, tok_sorted, steps


def kernel(x, mem, ln_in_g, ln_in_b, w_in, b_gate, hy_conv_w, hy_conv_b, hy_f1_w, hy_f1_b, hy_f2_w, hy_f2_b, hy_f3_w, hy_freq, hy_skip, sw_sink, mem_w_kv, w_br_hy, w_br_swa, w_br_mem, w_out, ln1_g, ln1_b, router_w, router_b, moe_w1, moe_b1, moe_w2, moe_b2, ln2_g, ln2_b):
    batch, seq, d = x.shape
    depth = w_in.shape[0]
    n = batch * seq
    width = hy_skip.shape[2]
    hy_cols = 3 * width
    q_cols = SW_HEADS * SW_HEAD_DIM
    kv_cols = SW_KV_HEADS * SW_HEAD_DIM
    mq_cols = MEM_HEADS * MEM_HEAD_DIM
    gate_cols = N_BRANCH * d
    mem_tokens = mem.shape[1]
    n_experts = router_w.shape[2]
    alpha = (2 * depth) ** 0.25
    assert batch == 2, "the long convolution packs exactly two batch elements into one complex signal"
    assert w_in.shape[2] == hy_cols + q_cols + 2 * kv_cols + mq_cols + gate_cols
    n1 = 2 * seq // DFT_N2

    cst = _dft_consts(n1, DFT_N2)
    zfeat = _filter_features(seq)
    cos_t, sin_t = _rope_tables(seq)
    swap = _rope_swap_matrix()

    o_q = hy_cols
    o_k = o_q + q_cols
    o_v = o_k + kv_cols
    o_mq = o_v + kv_cols
    o_g = o_mq + mq_cols
    dup = lambda w0: jnp.concatenate(
        [w_in[:, :, w0 + h * SW_HEAD_DIM:w0 + (h + 1) * SW_HEAD_DIM] for h in range(SW_KV_HEADS) for _ in range(2)],
        axis=2)
    w_proj = jnp.concatenate([w_in[:, :, o_g:o_g + gate_cols], w_in[:, :, :hy_cols], w_in[:, :, o_q:o_q + q_cols],
                              w_in[:, :, o_mq:o_mq + mq_cols], dup(o_k), dup(o_v)], axis=2).astype(BF)
    kw = 2 * kv_cols
    c_hy, c_q, c_mq = gate_cols, gate_cols + hy_cols, gate_cols + hy_cols + q_cols
    c_k = c_mq + mq_cols
    c_v = c_k + kw
    proj_cols = c_v + kw
    tn_proj = proj_cols // 4

    de = moe_w2.shape[2]
    b1g = moe_b1[..., 0::2].reshape(depth, n_experts, 1, de)
    b1l = moe_b1[..., 1::2].reshape(depth, n_experts, 1, de)
    b2 = moe_b2.reshape(depth, n_experts, 1, d)
    perm = _deinterleave_matrix()
    mem_bf = mem.reshape(batch * mem_tokens, d).astype(BF)

    filter_taps = lambda l, feats: _hyena_filter_taps(
        seq, width, hy_f1_w[l], hy_f1_b[l], hy_f2_w[l], hy_f2_b[l], hy_f3_w[l], hy_freq[l], feats)

    xf, xb = _ln_call(x.reshape(n, d), ln_in_g, ln_in_b)
    af = _hyena_filter_stage1(filter_taps(0, zfeat), seq, width, cst)
    for l in range(depth):
        proj = _mm_call(xb, w_proj[l], BF, 1024, tn_proj, "in_proj")

        hv, hx1, hx2 = _shortconv_call(proj, hy_conv_w[l], hy_conv_b[l], batch, seq, c_hy // hy_cols)
        wide = (2, n1 // 2, DFT_N2 * width)
        y_hy = hv.reshape(wide)
        for o, gx in enumerate((hx1, hx2)):
            y_hy = _hyena_conv(y_hy, gx.reshape(wide), af[o], hy_skip[l, o], cst, seq, width)
        y_hy = y_hy.reshape(n, width)

        y_sw = _attn_call(proj, c_q // q_cols, c_k // kw, q_cols, kw, proj[:, c_v:c_v + kw].T, cos_t, sin_t, swap,
                          sw_sink[l], batch, seq)

        memkv = _mm_call(mem_bf, mem_w_kv[l].astype(BF), BF, mem_tokens, 2 * mq_cols, "mem_kv")
        y_mem = _memattn_call(proj, c_mq // mq_cols, memkv, batch, seq, mem_tokens)

        xf, xb = _merge_call(xf, proj, 0, b_gate[l], y_hy, y_sw, y_mem, w_br_hy[l].astype(BF),
                             w_br_swa[l].astype(BF), w_br_mem[l].astype(BF), w_out[l].astype(BF),
                             ln1_g[l], ln1_b[l], alpha)

        gate, dest, tok_sorted, steps = _route(xf, router_w[l], router_b[l], n_experts)
        xs = xb.at[tok_sorted].get(mode="promise_in_bounds")
        if l + 1 < depth:
            taps = filter_taps(l + 1, zfeat + jnp.minimum(tok_sorted[0], 0).astype(F32))
            after_taps = jnp.minimum(lax.bitcast_convert_type(taps[0, 0, 0], jnp.uint16).astype(jnp.int32), 0)
            steps = (steps[0] + after_taps,) + steps[1:]
            af = _hyena_filter_stage1(taps, seq, width, cst)
        ys = _expert_call(steps, xs, moe_w1, b1g, b1l, moe_w2, b2, perm, l)
        yg = ys.at[dest].get(mode="promise_in_bounds")
        xf, xb = _combine_ln_call(xf, yg, gate, ln2_g[l], ln2_b[l], alpha)
    return xf.reshape(batch, seq, d)
```

```python
import functools
import math

import numpy as np
import jax
import jax.numpy as jnp
from jax import lax
from jax.experimental import pallas as pl
from jax.experimental.pallas import tpu as pltpu

BF = jnp.bfloat16
F32 = jnp.float32

HY_ORDER = 2
HY_SHORT = 3
HY_EMB = 33
HY_BANDS = (HY_EMB - 1) // 2
HY_FAST_DECAY = 0.3
HY_SLOW_DECAY = 1.5
HY_DECAY_TARGET = 1e-2
SW_HEADS = 8
SW_KV_HEADS = 2
SW_HEAD_DIM = 64
SW_WINDOW = 128
ROPE_THETA = 500000.0
ROPE_DIM = SW_HEAD_DIM // 4
MEM_HEADS = 4
MEM_HEAD_DIM = 128
N_BRANCH = 3
TOP_K = 4
SWIGLU_ALPHA = 1.702
SWIGLU_LIMIT = 7.0
MOE_BLOCK = 512
LN_EPS = 1e-5
NEG_INF = -1e30

LANES = 128
V7X_VMEM_BYTES = 64 * 1024 * 1024
VMEM_LIMIT = 48 * 1024 * 1024
VMEM_LIMIT_EXPERTS = 58 * 1024 * 1024
DFT_N2 = 128
SUBLANES = 8
FEAT_ROWS = -(-HY_EMB // SUBLANES) * SUBLANES


def _cparams(*sem):
    return pltpu.CompilerParams(dimension_semantics=sem, vmem_limit_bytes=VMEM_LIMIT)


def _ln_rows(r, g, b):
    mu = jnp.mean(r, axis=-1, keepdims=True)
    d = r - mu
    var = jnp.mean(d * d, axis=-1, keepdims=True)
    return d * lax.rsqrt(var + LN_EPS) * g + b


def _ln_kernel(x_ref, g_ref, b_ref, y_ref, yb_ref):
    y = _ln_rows(x_ref[...], g_ref[...], b_ref[...])
    y_ref[...] = y
    yb_ref[...] = y.astype(BF)


def _ln_call(x, g, b, tm=512):
    n, d = x.shape
    row = pl.BlockSpec((tm, d), lambda i: (i, 0))
    vec = pl.BlockSpec((1, d), lambda i: (0, 0))
    return pl.pallas_call(
        _ln_kernel, grid=(n // tm,), in_specs=[row, vec, vec], out_specs=[row, row],
        out_shape=[jax.ShapeDtypeStruct((n, d), F32), jax.ShapeDtypeStruct((n, d), BF)],
        compiler_params=_cparams("parallel"), name="ln_entry",
    )(x, g.reshape(1, d), b.reshape(1, d))


def _combine_ln_kernel(x_ref, y_ref, w_ref, g_ref, b_ref, o_ref, ob_ref, *, alpha):
    w = w_ref[...]
    f = y_ref[0].astype(F32) * w[:, 0:1]
    for k in range(1, TOP_K):
        f = f + y_ref[k].astype(F32) * w[:, k:k + 1]
    y = _ln_rows(alpha * x_ref[...] + f, g_ref[...], b_ref[...])
    o_ref[...] = y
    ob_ref[...] = y.astype(BF)


def _combine_ln_call(x, yg, w, g, b, alpha, tm=512):
    n, d = x.shape
    row = pl.BlockSpec((tm, d), lambda i: (i, 0))
    vec = pl.BlockSpec((1, d), lambda i: (0, 0))
    return pl.pallas_call(
        functools.partial(_combine_ln_kernel, alpha=alpha), grid=(n // tm,),
        in_specs=[row, pl.BlockSpec((TOP_K, tm, d), lambda i: (0, i, 0)),
                  pl.BlockSpec((tm, TOP_K), lambda i: (i, 0)), vec, vec],
        out_specs=[row, row],
        out_shape=[jax.ShapeDtypeStruct((n, d), F32), jax.ShapeDtypeStruct((n, d), BF)],
        compiler_params=_cparams("parallel"), name="moe_combine_ln",
    )(x, yg, w, g.reshape(1, d), b.reshape(1, d))


def _mm_kernel(a_ref, b_ref, o_ref):
    o_ref[...] = jnp.dot(a_ref[...], b_ref[...], preferred_element_type=F32).astype(o_ref.dtype)


def _mm_call(a, b, out_dtype, tm, tn, name):
    m, k = a.shape
    n = b.shape[1]
    return pl.pallas_call(
        _mm_kernel, grid=(n // tn, m // tm),
        in_specs=[pl.BlockSpec((tm, k), lambda j, i: (i, 0)), pl.BlockSpec((k, tn), lambda j, i: (0, j))],
        out_specs=pl.BlockSpec((tm, tn), lambda j, i: (i, j)),
        out_shape=jax.ShapeDtypeStruct((m, n), out_dtype),
        compiler_params=_cparams("parallel", "parallel"), name=name,
    )(a, b)


def _shortconv_kernel(zp_ref, z_ref, zn_ref, w_ref, b_ref, hv_ref, hx1_ref, hx2_ref, *, tl, halo, width):
    i = pl.program_id(1)
    nt = pl.num_programs(1)
    z = z_ref[...].astype(F32)
    prev_row = zp_ref[halo - 1:halo, :].astype(F32)
    next_row = zn_ref[0:1, :].astype(F32)
    prev_row = jnp.where(i == 0, 0.0, prev_row)
    next_row = jnp.where(i == nt - 1, 0.0, next_row)
    row = lax.broadcasted_iota(jnp.int32, z.shape, 0)
    zm1 = jnp.where(row == 0, prev_row, pltpu.roll(z, 1, 0))
    zp1 = jnp.where(row == tl - 1, next_row, pltpu.roll(z, tl - 1, 0))
    w = w_ref[...]
    out = zm1 * w[0:1] + b_ref[...] + z * w[1:2] + zp1 * w[2:3]
    hv_ref[...] = out[:, 0:width].astype(BF)
    hx1_ref[...] = out[:, width:2 * width].astype(BF)
    hx2_ref[...] = out[:, 2 * width:3 * width].astype(BF)


def _shortconv_call(proj, conv_w, conv_b, batch, seq, col_blk, tl=512, halo=16):
    n = proj.shape[0]
    c3 = conv_w.shape[1]
    width = c3 // 3
    nt = seq // tl
    hb = tl // halo
    nhb = seq // halo
    main = pl.BlockSpec((tl, c3), lambda b, i: (b * nt + i, col_blk))
    prev = pl.BlockSpec((halo, c3), lambda b, i: (b * nhb + jnp.maximum(i * hb - 1, 0), col_blk))
    nxt = pl.BlockSpec((halo, c3), lambda b, i: (b * nhb + jnp.minimum((i + 1) * hb, nhb - 1), col_blk))
    out = pl.BlockSpec((tl, width), lambda b, i: (b * nt + i, 0))
    return pl.pallas_call(
        functools.partial(_shortconv_kernel, tl=tl, halo=halo, width=width), grid=(batch, nt),
        in_specs=[prev, main, nxt, pl.BlockSpec((HY_SHORT, c3), lambda b, i: (0, 0)),
                  pl.BlockSpec((1, c3), lambda b, i: (0, 0))],
        out_specs=[out, out, out],
        out_shape=[jax.ShapeDtypeStruct((n, width), BF)] * 3,
        compiler_params=_cparams("parallel", "parallel"), name="hy_shortconv",
    )(proj, proj, proj, conv_w, conv_b.reshape(1, c3))


def _filter_kernel(z_ref, f1w_ref, f1b_ref, f2w_ref, f2b_ref, f3w_ref, freq_ref, adel_ref, k_ref,
                   *, tr, seq, n_inner):
    i = pl.program_id(0)
    hp = lax.Precision.HIGHEST
    fr = freq_ref[...]
    h = jnp.sin(fr * (jnp.dot(f1w_ref[...], z_ref[...], precision=hp, preferred_element_type=F32)
                      + f1b_ref[...]))
    for j in range(n_inner):
        h = jnp.sin(fr * (jnp.dot(f2w_ref[j], h, precision=hp, preferred_element_type=F32) + f2b_ref[j]))
    k = lax.dot_general(h.astype(BF), f3w_ref[...], (((0,), (0,)), ((), ())),
                        preferred_element_type=F32)
    width = adel_ref.shape[1]
    row = i * tr + lax.broadcasted_iota(jnp.int32, (tr, width), 0)
    pos = jnp.where(row < seq, row, 2 * seq - row)
    win = jnp.exp(-(pos.astype(F32) * (1.0 / (seq - 1))) * adel_ref[...])
    win = jnp.where(row == seq, 0.0, win)
    for o in range(HY_ORDER):
        k_ref[o] = (k[:, o * width:(o + 1) * width] * win).astype(k_ref.dtype)


def _filter_call(zfeat_t, f1w_t, f1b, f2w_t, f2b, f3w_dir, freq, adel, seq, tr=1024):
    zr, n2l = zfeat_t.shape
    hid = f1w_t.shape[0]
    n_inner = f2w_t.shape[0]
    ow = f3w_dir.shape[2]
    width = adel.shape[1]
    half = (n2l // tr) // 2
    full = lambda *shape: pl.BlockSpec(shape, lambda i: (0,) * len(shape))
    return pl.pallas_call(
        functools.partial(_filter_kernel, tr=tr, seq=seq, n_inner=n_inner), grid=(n2l // tr,),
        in_specs=[pl.BlockSpec((zr, tr), lambda i: (0, i)), full(hid, zr), full(hid, 1),
                  full(n_inner, hid, hid), full(n_inner, hid, 1),
                  pl.BlockSpec((None, hid, ow), lambda i: (i // half, 0, 0)),
                  full(hid, 1), full(1, width)],
        out_specs=pl.BlockSpec((HY_ORDER, tr, width), lambda i: (0, i, 0)),
        out_shape=jax.ShapeDtypeStruct((HY_ORDER, n2l, width), BF),
        compiler_params=_cparams("parallel"), name="hy_filter",
    )(zfeat_t, f1w_t, f1b, f2w_t, f2b, f3w_dir, freq, adel)


def _dft_consts(n1, n2):
    n = n1 * n2
    h1 = n1 // 2
    k1 = np.arange(n1)[:, None]
    ang = -2.0 * np.pi * (k1 * np.arange(h1)[None, :]) / n1
    cr, ci = np.cos(ang), np.sin(ang)
    e1 = np.block([[cr, -ci], [ci, cr]])
    angf = -2.0 * np.pi * (k1 * np.arange(n1)[None, :]) / n1
    ef = np.concatenate([np.cos(angf), np.sin(angf)], axis=0)
    a2 = -2.0 * np.pi * (np.arange(n2)[:, None] * np.arange(n2)[None, :]) / n2
    f2r, f2i = np.cos(a2), np.sin(a2)
    dm = np.block([[f2r, f2i], [-f2i, f2r]])
    at = -2.0 * np.pi * (np.arange(n1)[:, None] * np.arange(n2)[None, :]) / n
    twr, twi = np.cos(at), np.sin(at)
    ab = 2.0 * np.pi * (np.arange(h1)[:, None] * np.arange(n1)[None, :]) / n1
    br, bi = np.cos(ab) / n, np.sin(ab) / n
    f = lambda a: jnp.asarray(a, dtype=F32)
    return dict(e1=f(e1), ef=f(ef), f2r=f(f2r), f2i=f(f2i), dm=f(dm),
                twr=f(twr.reshape(n1, 1, n2)), twi=f(twi.reshape(n1, 1, n2)),
                ctr=f(twr.T.reshape(n2, 1, n1)), cti=f(-twi.T.reshape(n2, 1, n1)),
                br=f(br), bi=f(bi))


def _dft_s1_kernel(e_ref, top_ref, bot_ref, a_ref):
    rhs = jnp.concatenate([top_ref[...], bot_ref[...]], axis=0).astype(BF)
    res = jnp.dot(e_ref[...], rhs, preferred_element_type=F32)
    n1 = a_ref.shape[1]
    a_ref[0] = res[:n1].astype(a_ref.dtype)
    a_ref[1] = res[n1:].astype(a_ref.dtype)


def _dft_s1_call(e_mat, x3, top_idx, bot_idx, n1, tn=4096):
    h1, cols = x3.shape[1], x3.shape[2]
    return pl.pallas_call(
        _dft_s1_kernel, grid=(cols // tn,),
        in_specs=[pl.BlockSpec((2 * n1, 2 * h1), lambda j: (0, 0)),
                  pl.BlockSpec((None, h1, tn), lambda j: (top_idx, 0, j)),
                  pl.BlockSpec((None, h1, tn), lambda j: (bot_idx, 0, j))],
        out_specs=pl.BlockSpec((2, n1, tn), lambda j: (0, 0, j)),
        out_shape=jax.ShapeDtypeStruct((2, n1, cols), BF),
        compiler_params=_cparams("parallel"), name="hy_dft_stage1",
    )(e_mat.astype(BF), x3, x3)


def _twiddled_f2(f2r, f2i, tr, ti):
    gr = f2r * tr - f2i * ti
    gi = f2r * ti + f2i * tr
    return jnp.concatenate([jnp.concatenate([gr, -gi], axis=1), jnp.concatenate([gi, gr], axis=1)], axis=0)


def _dft_s2_conv_kernel(a_ref, af_ref, f2r_ref, f2i_ref, dm_ref, twr_ref, twi_ref, b_ref, *, kb):
    n2 = f2r_ref.shape[0]
    dm = dm_ref[...]
    for j in range(kb):
        gm = _twiddled_f2(f2r_ref[...], f2i_ref[...], twr_ref[j], twi_ref[j]).astype(BF)
        rhs = jnp.concatenate([a_ref[0, j], a_ref[1, j]], axis=0)
        x = jnp.dot(gm, rhs, preferred_element_type=F32)
        kf = jnp.dot(gm, jnp.concatenate([af_ref[0, j], af_ref[1, j]], axis=0),
                     preferred_element_type=F32)
        xr, xi = x[:n2], x[n2:]
        kr, ki = kf[:n2], kf[n2:]
        y = jnp.concatenate([xr * kr - xi * ki, xr * ki + xi * kr], axis=0).astype(BF)
        bm = jnp.dot(dm, y, preferred_element_type=F32)
        b_ref[0, j] = bm[:n2].astype(b_ref.dtype)
        b_ref[1, j] = bm[n2:].astype(b_ref.dtype)


def _dft_s2_conv_call(a4, af4, cst, kb=16):
    _, n1, n2, c = a4.shape
    blk = pl.BlockSpec((2, kb, n2, c), lambda i: (0, i, 0, 0))
    sq = pl.BlockSpec((n2, n2), lambda i: (0, 0))
    tw = pl.BlockSpec((kb, 1, n2), lambda i: (i, 0, 0))
    return pl.pallas_call(
        functools.partial(_dft_s2_conv_kernel, kb=kb), grid=(n1 // kb,),
        in_specs=[blk, blk, sq, sq, pl.BlockSpec((2 * n2, 2 * n2), lambda i: (0, 0)), tw, tw],
        out_specs=blk, out_shape=jax.ShapeDtypeStruct(a4.shape, BF),
        compiler_params=_cparams("parallel"), name="hy_dft_stage2_conv",
    )(a4, af4, cst["f2r"], cst["f2i"], cst["dm"].astype(BF), cst["twr"], cst["twi"])


def _dft_s1inv_kernel(b_ref, br_ref, bi_ref, ctr_ref, cti_ref, u_ref, gx_ref, skip_ref, o_ref, *, ng, c):
    h1 = br_ref.shape[0]
    skip = skip_ref[...]
    for j in range(ng):
        tr, ti = ctr_ref[j], cti_ref[j]
        mr = br_ref[...] * tr - bi_ref[...] * ti
        mi = br_ref[...] * ti + bi_ref[...] * tr
        em = jnp.concatenate([jnp.concatenate([mr, -mi], axis=1),
                              jnp.concatenate([mi, mr], axis=1)], axis=0).astype(BF)
        y = jnp.dot(em, b_ref[:, j * c:(j + 1) * c], preferred_element_type=F32)
        for b in range(2):
            u = u_ref[b, :, j * c:(j + 1) * c].astype(F32)
            g = gx_ref[b, :, j * c:(j + 1) * c].astype(F32)
            o_ref[b, :, j * c:(j + 1) * c] = (g * (y[b * h1:(b + 1) * h1] + skip * u)).astype(o_ref.dtype)


def _dft_s1inv_call(b2, u3, gx3, skip, cst, c, ng=8):
    rows, cols = b2.shape
    h1 = u3.shape[1]
    n2 = cols // c
    n1 = rows // 2
    ub = pl.BlockSpec((2, h1, ng * c), lambda i: (0, 0, i))
    cb = pl.BlockSpec((h1, n1), lambda i: (0, 0))
    tw = pl.BlockSpec((ng, 1, n1), lambda i: (i, 0, 0))
    return pl.pallas_call(
        functools.partial(_dft_s1inv_kernel, ng=ng, c=c), grid=(n2 // ng,),
        in_specs=[pl.BlockSpec((rows, ng * c), lambda i: (0, i)), cb, cb, tw, tw, ub, ub,
                  pl.BlockSpec((1, c), lambda i: (0, 0))],
        out_specs=ub, out_shape=jax.ShapeDtypeStruct(u3.shape, BF),
        compiler_params=_cparams("parallel"), name="hy_dft_stage1_inverse",
    )(b2, cst["br"], cst["bi"], cst["ctr"], cst["cti"], u3, gx3, skip.reshape(1, c))


def _filter_features(seq):
    t01 = jnp.linspace(0.0, 1.0, seq, dtype=F32)[:, None]
    w = (2.0 * math.pi) * jnp.arange(seq, dtype=F32)[:, None] / seq
    bands = jnp.linspace(1e-4, HY_BANDS - 1, HY_BANDS, dtype=F32)
    z = jnp.concatenate([t01, jnp.cos(bands * w), -jnp.sin(bands * w)], axis=-1)
    z2 = jnp.concatenate([z, z[:1], z[1:][::-1]], axis=0)
    return jnp.pad(z2, ((0, 0), (0, FEAT_ROWS - HY_EMB))).T


def _hyena_filter_stage1(seq, width, f1_w, f1_b, f2_w, f2_b, f3_w, freq, zfeat_t, cst):
    hid = f1_w.shape[1]
    n1 = 2 * seq // DFT_N2
    max_decay = math.log(HY_DECAY_TARGET) / HY_FAST_DECAY
    min_decay = math.log(HY_DECAY_TARGET) / HY_SLOW_DECAY
    adel = jnp.abs(jnp.linspace(min_decay, max_decay, width, dtype=F32)).reshape(1, width)
    f1w_t = jnp.pad(f1_w, ((0, FEAT_ROWS - HY_EMB), (0, 0))).T
    f3d = f3_w.reshape(hid, HY_ORDER, 2, width).transpose(2, 0, 1, 3).reshape(2, hid, HY_ORDER * width)
    k3 = _filter_call(zfeat_t, f1w_t, f1_b.reshape(hid, 1), f2_w.transpose(0, 2, 1), f2_b.reshape(-1, hid, 1),
                      f3d.astype(BF), freq.reshape(hid, 1), adel, seq)
    kw = k3.reshape(HY_ORDER * 2, n1 // 2, DFT_N2 * width)
    return [_dft_s1_call(cst["ef"], kw, 2 * o, 2 * o + 1, n1).reshape(2, n1, DFT_N2, width)
            for o in range(HY_ORDER)]


def _hyena_conv(u3, gx3, af, skip, cst, seq, width):
    n1 = 2 * seq // DFT_N2
    a = _dft_s1_call(cst["e1"], u3, 0, 1, n1)
    bm = _dft_s2_conv_call(a.reshape(2, n1, DFT_N2, width), af, cst)
    return _dft_s1inv_call(bm.reshape(2 * n1, DFT_N2 * width), u3, gx3, skip, cst, width)


def _rope_tables(seq):
    half = ROPE_DIM // 2
    inv = jnp.power(jnp.float32(ROPE_THETA), -jnp.arange(half, dtype=F32) * (2.0 / ROPE_DIM))
    ang = jnp.arange(seq, dtype=jnp.int32).astype(F32)[:, None] * inv
    cos, sin = jnp.cos(ang), jnp.sin(ang)
    rest = SW_HEAD_DIM - ROPE_DIM
    ones = jnp.ones((seq, rest), F32)
    zeros = jnp.zeros((seq, rest), F32)
    ch = jnp.concatenate([cos, cos, ones], axis=1)
    sh = jnp.concatenate([-sin, sin, zeros], axis=1)
    reps = LANES // SW_HEAD_DIM
    return jnp.tile(ch, (1, reps)), jnp.tile(sh, (1, reps))


def _rope_swap_matrix():
    half = ROPE_DIM // 2
    p = np.zeros((LANES, LANES), np.float32)
    for j in range(LANES):
        d = j % SW_HEAD_DIM
        if d < half:
            p[j + half, j] = 1.0
        elif d < ROPE_DIM:
            p[j - half, j] = 1.0
    return jnp.asarray(p)


def _attn_kernel(sink_ref, q_ref, kp_ref, km_ref, kn_ref, vp_ref, vm_ref, vn_ref, cp_ref, cm_ref, cn_ref,
                 sp_ref, sm_ref, sn_ref, swap_ref, o_ref, *, tq, seq):
    blk = SW_WINDOW
    i = pl.program_id(1)
    swap = swap_ref[...]

    def rot(x, c, s):
        return x.astype(F32) * c + jnp.dot(x, swap, preferred_element_type=F32) * s

    cwin = jnp.concatenate([cp_ref[...], cm_ref[...], cn_ref[...]], axis=0)
    swin = jnp.concatenate([sp_ref[...], sm_ref[...], sn_ref[...]], axis=0)
    kraw = jnp.concatenate([kp_ref[...], km_ref[...], kn_ref[...]], axis=0)
    kwin = jnp.concatenate([rot(kraw[:, h * LANES:(h + 1) * LANES], cwin, swin).astype(BF)
                            for h in range(SW_KV_HEADS)], axis=1)
    q_scale = SW_HEAD_DIM ** -0.5
    qrot = [(rot(q_ref[:, g * LANES:(g + 1) * LANES], cm_ref[...], sm_ref[...]) * q_scale).astype(BF)
            for g in range(q_ref.shape[1] // LANES)]
    vwin_t = jnp.concatenate([vp_ref[...], vm_ref[...], vn_ref[...]], axis=1)
    lo = lax.broadcasted_iota(jnp.int32, (blk, LANES), 1) < SW_HEAD_DIM
    zero = jnp.zeros((), BF)
    per_group = LANES // SW_HEAD_DIM
    groups_per_kv = (SW_HEADS // SW_KV_HEADS) // per_group
    stack = groups_per_kv * per_group
    w_idx = lax.broadcasted_iota(jnp.int32, (3 * blk, stack * blk), 0)
    a_idx = lax.broadcasted_iota(jnp.int32, (3 * blk, stack * blk), 1) & (blk - 1)
    rel = w_idx - a_idx
    for jb in range(tq // blk):
        kpos = i * tq + (jb - 1) * blk + w_idx
        bad = (rel < 0) | (rel > 2 * SW_WINDOW) | (kpos < 0) | (kpos >= seq)
        for h in range(SW_KV_HEADS):
            parts, sinks = [], []
            for gg in range(groups_per_kv):
                g = h * groups_per_kv + gg
                qg = qrot[g][jb * blk:(jb + 1) * blk]
                parts += [jnp.where(lo, qg, zero), jnp.where(lo, zero, qg)]
                sinks += [jnp.full((1, blk), sink_ref[g * per_group + par], F32) for par in range(per_group)]
            qs = jnp.concatenate(parts, axis=0)
            sk = jnp.concatenate(sinks, axis=1)
            kh = kwin[jb * blk:(jb + 3) * blk, h * LANES:(h + 1) * LANES]
            vt = vwin_t[h * LANES:(h + 1) * LANES, jb * blk:(jb + 3) * blk]
            s = lax.dot_general(kh, qs, (((1,), (1,)), ((), ())), preferred_element_type=F32)
            s = jnp.where(bad, NEG_INF, s)
            m = jnp.maximum(jnp.max(s, axis=0, keepdims=True), sk)
            p = jnp.exp(s - m)
            denom = jnp.sum(p, axis=0, keepdims=True) + jnp.exp(sk - m)
            ot = jnp.dot(vt, p.astype(BF), preferred_element_type=F32) / denom
            o = ot.T
            for gg in range(groups_per_kv):
                g = h * groups_per_kv + gg
                oa = o[(gg * per_group) * blk:(gg * per_group + 1) * blk]
                ob = o[(gg * per_group + 1) * blk:(gg * per_group + 2) * blk]
                o_ref[jb * blk:(jb + 1) * blk, g * LANES:(g + 1) * LANES] = jnp.where(lo, oa, ob).astype(BF)


def _attn_call(proj, q_blk, k_blk, qw, kw, v_t, cos_t, sin_t, swap, sink, batch, seq, tq=512):
    n = proj.shape[0]
    blk = SW_WINDOW
    nt = seq // tq
    r = tq // blk
    nb = seq // blk
    prev_p = lambda i: jnp.maximum(i * r - 1, 0)
    next_p = lambda i: jnp.minimum((i + 1) * r, nb - 1)
    tab = lambda rows, f: pl.BlockSpec((rows, LANES), lambda b, i, s: (f(i), 0))
    tabs = [tab(blk, prev_p), tab(tq, lambda i: i), tab(blk, next_p)]
    grid_spec = pltpu.PrefetchScalarGridSpec(
        num_scalar_prefetch=1, grid=(batch, nt),
        in_specs=[pl.BlockSpec((tq, qw), lambda b, i, s: (b * nt + i, q_blk)),
                  pl.BlockSpec((blk, kw), lambda b, i, s: (b * nb + prev_p(i), k_blk)),
                  pl.BlockSpec((tq, kw), lambda b, i, s: (b * nt + i, k_blk)),
                  pl.BlockSpec((blk, kw), lambda b, i, s: (b * nb + next_p(i), k_blk)),
                  pl.BlockSpec((kw, blk), lambda b, i, s: (0, b * nb + prev_p(i))),
                  pl.BlockSpec((kw, tq), lambda b, i, s: (0, b * nt + i)),
                  pl.BlockSpec((kw, blk), lambda b, i, s: (0, b * nb + next_p(i)))]
                 + tabs + tabs + [pl.BlockSpec((LANES, LANES), lambda b, i, s: (0, 0))],
        out_specs=pl.BlockSpec((tq, qw), lambda b, i, s: (b * nt + i, 0)))
    return pl.pallas_call(
        functools.partial(_attn_kernel, tq=tq, seq=seq), grid_spec=grid_spec,
        out_shape=jax.ShapeDtypeStruct((n, qw), BF),
        compiler_params=_cparams("parallel", "parallel"), name="sw_attention",
    )(sink, proj, proj, proj, proj, v_t, v_t, v_t, cos_t, cos_t, cos_t, sin_t, sin_t, sin_t, swap.astype(BF))


def _memattn_kernel(q_ref, mk_ref, mv_ref, o_ref):
    scale = MEM_HEAD_DIM ** -0.5
    for h in range(MEM_HEADS):
        sl = slice(h * MEM_HEAD_DIM, (h + 1) * MEM_HEAD_DIM)
        s = lax.dot_general(q_ref[:, sl], mk_ref[:, sl], (((1,), (1,)), ((), ())),
                            preferred_element_type=F32) * scale
        m = jnp.max(s, axis=-1, keepdims=True)
        p = jnp.exp(s - m)
        denom = jnp.sum(p, axis=-1, keepdims=True)
        o = jnp.dot(p.astype(BF), mv_ref[:, sl], preferred_element_type=F32) / denom
        o_ref[:, sl] = o.astype(BF)


def _memattn_call(proj, q_blk, memkv, batch, seq, mem_tokens, tm=1024):
    n = proj.shape[0]
    w = MEM_HEADS * MEM_HEAD_DIM
    nt = seq // tm
    return pl.pallas_call(
        _memattn_kernel, grid=(batch, nt),
        in_specs=[pl.BlockSpec((tm, w), lambda b, i: (b * nt + i, q_blk)),
                  pl.BlockSpec((mem_tokens, w), lambda b, i: (b, 0)),
                  pl.BlockSpec((mem_tokens, w), lambda b, i: (b, 1))],
        out_specs=pl.BlockSpec((tm, w), lambda b, i: (b * nt + i, 0)),
        out_shape=jax.ShapeDtypeStruct((n, w), BF),
        compiler_params=_cparams("parallel", "parallel"), name="mem_attention",
    )(proj, memkv, memkv)


def _merge_kernel(x_ref, gl_ref, bg_ref, yh_ref, ys_ref, ym_ref, wh_ref, ws_ref, wm_ref, wo_ref,
                  g_ref, b_ref, o_ref, ob_ref, *, alpha, d):
    merged = None
    for br, (y_ref, w_ref) in enumerate(((yh_ref, wh_ref), (ys_ref, ws_ref), (ym_ref, wm_ref))):
        gate = jax.nn.sigmoid(gl_ref[:, br * d:(br + 1) * d].astype(F32) + bg_ref[br:br + 1, :])
        t = gate * jnp.dot(y_ref[...], w_ref[...], preferred_element_type=F32)
        merged = t if merged is None else merged + t
    h = jnp.dot(merged.astype(BF), wo_ref[...], preferred_element_type=F32)
    y = _ln_rows(alpha * x_ref[...] + h, g_ref[...], b_ref[...])
    o_ref[...] = y
    ob_ref[...] = y.astype(BF)


def _merge_call(x, proj, gate_blk, b_gate, y_hy, y_sw, y_mem, w_hy, w_sw, w_mem, w_out, g, b, alpha, tm=512):
    n, d = x.shape
    row = lambda w: pl.BlockSpec((tm, w), lambda i: (i, 0))
    full = lambda a: pl.BlockSpec(a.shape, lambda i: (0,) * a.ndim)
    g2, b2 = g.reshape(1, d), b.reshape(1, d)
    return pl.pallas_call(
        functools.partial(_merge_kernel, alpha=alpha, d=d), grid=(n // tm,),
        in_specs=[row(d), pl.BlockSpec((tm, N_BRANCH * d), lambda i: (i, gate_blk)), full(b_gate),
                  row(y_hy.shape[1]), row(y_sw.shape[1]), row(y_mem.shape[1]),
                  full(w_hy), full(w_sw), full(w_mem), full(w_out), full(g2), full(b2)],
        out_specs=[row(d), row(d)],
        out_shape=[jax.ShapeDtypeStruct((n, d), F32), jax.ShapeDtypeStruct((n, d), BF)],
        compiler_params=_cparams("parallel"), name="merge_outproj_ln",
    )(x, proj, b_gate, y_hy, y_sw, y_mem, w_hy, w_sw, w_mem, w_out, g2, b2)


def _deinterleave_matrix():
    perm = np.zeros((2 * LANES, 2 * LANES), np.float32)
    perm[2 * np.arange(LANES), np.arange(LANES)] = 1.0
    perm[2 * np.arange(LANES) + 1, LANES + np.arange(LANES)] = 1.0
    return jnp.asarray(perm).astype(BF)


def _expert_kernel(sg_ref, st_ref, lo_ref, hi_ref, x_ref, w1_ref, b1g_ref, b1l_ref, w2_ref, b2_ref, perm_ref,
                   y_ref, w1g_ref, w1l_ref, w2b_ref):
    s = pl.program_id(0)
    lo, hi = lo_ref[s], hi_ref[s]
    prev = jnp.maximum(s - 1, 0)

    @pl.when((s == 0) | (sg_ref[s] != sg_ref[prev]))
    def _():
        p = perm_ref[...]
        for c in range(w1g_ref.shape[1] // LANES):
            r = jnp.dot(w1_ref[:, 2 * c * LANES:2 * (c + 1) * LANES].astype(BF), p, preferred_element_type=F32)
            w1g_ref[:, c * LANES:(c + 1) * LANES] = r[:, :LANES].astype(BF)
            w1l_ref[:, c * LANES:(c + 1) * LANES] = r[:, LANES:].astype(BF)
        w2b_ref[...] = w2_ref[...].astype(BF)

    @pl.when(hi > lo)
    def _():
        x = x_ref[...]
        hg = jnp.dot(x, w1g_ref[...], preferred_element_type=F32) + b1g_ref[...]
        hl = jnp.dot(x, w1l_ref[...], preferred_element_type=F32) + b1l_ref[...]
        hg = jnp.minimum(hg, SWIGLU_LIMIT)
        hl = jnp.clip(hl, -SWIGLU_LIMIT, SWIGLU_LIMIT)
        act = hg * jax.nn.sigmoid(SWIGLU_ALPHA * hg) * (hl + 1.0)
        y = jnp.dot(act.astype(BF), w2b_ref[...], preferred_element_type=F32) + b2_ref[...]
        whole = (lo == 0) & (hi == y.shape[0])
        first_visit = (s == 0) | (st_ref[s] != st_ref[prev])

        @pl.when(whole)
        def _():
            y_ref[...] = y.astype(y_ref.dtype)

        def masked(prev):
            row = lax.broadcasted_iota(jnp.int32, y.shape, 0)
            y_ref[...] = jnp.where((row >= lo) & (row < hi), y, prev).astype(y_ref.dtype)

        @pl.when(jnp.logical_not(whole) & first_visit)
        def _():
            masked(jnp.zeros_like(y))

        @pl.when(jnp.logical_not(whole) & jnp.logical_not(first_visit))
        def _():
            masked(y_ref[...].astype(F32))


def _expert_call(steps, xs, w1, b1g, b1l, w2, b2, perm, layer):
    a, d = xs.shape
    de = w2.shape[2]
    n_steps = steps[0].shape[0]
    wspec = lambda r, c: pl.BlockSpec((None, None, r, c), lambda i, sg, st, lo, hi: (layer, sg[i], 0, 0))
    rows = pl.BlockSpec((MOE_BLOCK, d), lambda i, sg, st, lo, hi: (st[i], 0))
    grid_spec = pltpu.PrefetchScalarGridSpec(
        num_scalar_prefetch=4, grid=(n_steps,),
        in_specs=[rows, wspec(d, 2 * de), wspec(1, de), wspec(1, de), wspec(de, d), wspec(1, d),
                  pl.BlockSpec(perm.shape, lambda i, sg, st, lo, hi: (0, 0))],
        out_specs=rows,
        scratch_shapes=[pltpu.VMEM((d, de), BF), pltpu.VMEM((d, de), BF), pltpu.VMEM((de, d), BF)])
    return pl.pallas_call(
        _expert_kernel, grid_spec=grid_spec, out_shape=jax.ShapeDtypeStruct((a, d), BF),
        compiler_params=pltpu.CompilerParams(dimension_semantics=("arbitrary",),
                                             vmem_limit_bytes=VMEM_LIMIT_EXPERTS), name="moe_experts",
    )(*steps, xs, w1, b1g, b1l, w2, b2, perm)


def _router_kernel(x_ref, wh_ref, wl_ref, b_ref, tri_ref, ir_ref, gate_ref, cnt_ref, carry_ref):
    @pl.when(pl.program_id(0) == 0)
    def _():
        carry_ref[...] = jnp.zeros_like(carry_ref)

    x = x_ref[...]
    xh = x.astype(BF)
    xl = (x - xh.astype(F32)).astype(BF)
    wh = wh_ref[...]
    logits = (jnp.dot(xh, wh, preferred_element_type=F32) + jnp.dot(xl, wh, preferred_element_type=F32)
              + jnp.dot(xh, wl_ref[...], preferred_element_type=F32)) + b_ref[...]
    lane_i = lax.broadcasted_iota(jnp.int32, logits.shape, 1)
    lane = lane_i.astype(F32)
    cur = logits
    vals, hots, idxs = [], [], []
    for k in range(TOP_K):
        m = jnp.max(cur, axis=-1, keepdims=True)
        idx = jnp.min(jnp.where(cur == m, lane, float(LANES)), axis=-1, keepdims=True)
        hot = lane == idx
        cur = jnp.where(hot, NEG_INF, cur)
        vals.append(m)
        hots.append(hot.astype(F32))
        idxs.append(idx)
    ex = [jnp.exp(v - vals[0]) for v in vals]
    tot = ex[0]
    for e in ex[1:]:
        tot = tot + e
    for k in range(TOP_K):
        gate_ref[:, k:k + 1] = ex[k] / tot
    chosen = hots[0]
    for h in hots[1:]:
        chosen = chosen + h
    before = jnp.dot(tri_ref[...], chosen.astype(BF), preferred_element_type=F32) + carry_ref[...]
    cols = jnp.zeros(logits.shape, F32)
    for k in range(TOP_K):
        cols = jnp.where(lane_i == k, idxs[k], cols)
        cols = jnp.where(lane_i == TOP_K + k, jnp.sum(hots[k] * before, axis=-1, keepdims=True), cols)
    ir_ref[...] = cols.T[:2 * TOP_K].astype(jnp.int32)
    carry_ref[...] = carry_ref[...] + jnp.sum(chosen, axis=0, keepdims=True)
    cnt_ref[...] = carry_ref[...].astype(jnp.int32)


def _router_call(x, router_w, router_b, tm=512):
    n, d = x.shape
    ne = router_w.shape[1]
    assert ne <= LANES
    w = jnp.pad(router_w, ((0, 0), (0, LANES - ne)))
    wh = w.astype(BF)
    wl = (w - wh.astype(F32)).astype(BF)
    b = jnp.pad(router_b, (0, LANES - ne), constant_values=NEG_INF).reshape(1, LANES)
    tri = jnp.asarray(np.tril(np.ones((tm, tm), np.float32), -1)).astype(BF)
    wspec = pl.BlockSpec((d, LANES), lambda i: (0, 0))
    return pl.pallas_call(
        _router_kernel, grid=(n // tm,),
        in_specs=[pl.BlockSpec((tm, d), lambda i: (i, 0)), wspec, wspec,
                  pl.BlockSpec((1, LANES), lambda i: (0, 0)), pl.BlockSpec((tm, tm), lambda i: (0, 0))],
        out_specs=[pl.BlockSpec((2 * TOP_K, tm), lambda i: (0, i)), pl.BlockSpec((tm, TOP_K), lambda i: (i, 0)),
                   pl.BlockSpec((1, LANES), lambda i: (0, 0))],
        out_shape=[jax.ShapeDtypeStruct((2 * TOP_K, n), jnp.int32), jax.ShapeDtypeStruct((n, TOP_K), F32),
                   jax.ShapeDtypeStruct((1, LANES), jnp.int32)],
        scratch_shapes=[pltpu.VMEM((1, LANES), F32)],
        compiler_params=_cparams("arbitrary"), name="moe_router",
    )(x, wh, wl, b, tri)


def _route(x, router_w, router_b, n_experts):
    n = x.shape[0]
    ir, gate, cnt = _router_call(x, router_w, router_b)
    top_i, rank = ir[:TOP_K], ir[TOP_K:]
    a = n * TOP_K
    assert a % MOE_BLOCK == 0 and n_experts * a < 2 ** 31
    tok = jnp.arange(n, dtype=jnp.int32)[None, :]
    slot = jnp.arange(TOP_K, dtype=jnp.int32)[:, None]
    skey = jnp.sort((top_i * a + tok * TOP_K + slot).reshape(-1))
    tok_sorted = (skey % a) // TOP_K
    experts = jnp.arange(n_experts, dtype=jnp.int32)
    counts = cnt[0, :n_experts]
    ends = jnp.cumsum(counts)
    starts = ends - counts
    pos = rank
    for e in range(n_experts):
        pos = pos + jnp.where(top_i == e, starts[e], 0)
    n_tiles = a // MOE_BLOCK
    first_tile = starts // MOE_BLOCK
    ntile = jnp.where(counts > 0, (ends - 1) // MOE_BLOCK - first_tile + 1, 0)
    cum = jnp.cumsum(ntile)
    n_steps = n_tiles + n_experts - 1
    s = jnp.arange(n_steps, dtype=jnp.int32)
    g = jnp.minimum(jnp.sum((s[:, None] >= cum[None, :]).astype(jnp.int32), axis=1), n_experts - 1)
    live = s < cum[-1]
    pick = g[:, None] == experts[None, :]
    at_g = lambda v: jnp.sum(jnp.where(pick, v[None, :], 0), axis=1)
    tile = jnp.where(live, at_g(first_tile) + s - at_g(cum - ntile), n_tiles - 1)
    lo = jnp.clip(at_g(starts) - tile * MOE_BLOCK, 0, MOE_BLOCK)
    hi = jnp.where(live, jnp.clip(at_g(ends) - tile * MOE_BLOCK, 0, MOE_BLOCK), lo)
    g = jnp.where(live, g, jnp.max(jnp.where(live, g, 0)))
    steps = tuple(v.astype(jnp.int32) for v in (g, tile, lo, hi))
    return gate, pos, tok_sorted, steps


def kernel(x, mem, ln_in_g, ln_in_b, w_in, b_gate, hy_conv_w, hy_conv_b, hy_f1_w, hy_f1_b, hy_f2_w, hy_f2_b, hy_f3_w, hy_freq, hy_skip, sw_sink, mem_w_kv, w_br_hy, w_br_swa, w_br_mem, w_out, ln1_g, ln1_b, router_w, router_b, moe_w1, moe_b1, moe_w2, moe_b2, ln2_g, ln2_b):
    batch, seq, d = x.shape
    depth = w_in.shape[0]
    n = batch * seq
    width = hy_skip.shape[2]
    hy_cols = 3 * width
    q_cols = SW_HEADS * SW_HEAD_DIM
    kv_cols = SW_KV_HEADS * SW_HEAD_DIM
    mq_cols = MEM_HEADS * MEM_HEAD_DIM
    gate_cols = N_BRANCH * d
    mem_tokens = mem.shape[1]
    n_experts = router_w.shape[2]
    alpha = (2 * depth) ** 0.25
    assert batch == 2, "the long convolution packs exactly two batch elements into one complex signal"
    assert w_in.shape[2] == hy_cols + q_cols + 2 * kv_cols + mq_cols + gate_cols
    n1 = 2 * seq // DFT_N2

    cst = _dft_consts(n1, DFT_N2)
    zfeat = _filter_features(seq)
    cos_t, sin_t = _rope_tables(seq)
    swap = _rope_swap_matrix()

    o_q = hy_cols
    o_k = o_q + q_cols
    o_v = o_k + kv_cols
    o_mq = o_v + kv_cols
    o_g = o_mq + mq_cols
    dup = lambda w0: jnp.concatenate(
        [w_in[:, :, w0 + h * SW_HEAD_DIM:w0 + (h + 1) * SW_HEAD_DIM] for h in range(SW_KV_HEADS) for _ in range(2)],
        axis=2)
    w_proj = jnp.concatenate([w_in[:, :, o_g:o_g + gate_cols], w_in[:, :, :hy_cols], w_in[:, :, o_q:o_q + q_cols],
                              w_in[:, :, o_mq:o_mq + mq_cols], dup(o_k), dup(o_v)], axis=2).astype(BF)
    kw = 2 * kv_cols
    c_hy, c_q, c_mq = gate_cols, gate_cols + hy_cols, gate_cols + hy_cols + q_cols
    c_k = c_mq + mq_cols
    c_v = c_k + kw
    proj_cols = c_v + kw
    tn_proj = proj_cols // 4

    de = moe_w2.shape[2]
    b1g = moe_b1[..., 0::2].reshape(depth, n_experts, 1, de)
    b1l = moe_b1[..., 1::2].reshape(depth, n_experts, 1, de)
    b2 = moe_b2.reshape(depth, n_experts, 1, d)
    perm = _deinterleave_matrix()
    mem_bf = mem.reshape(batch * mem_tokens, d).astype(BF)

    filter_stage1 = lambda l, feats: _hyena_filter_stage1(
        seq, width, hy_f1_w[l], hy_f1_b[l], hy_f2_w[l], hy_f2_b[l], hy_f3_w[l], hy_freq[l], feats, cst)

    xf, xb = _ln_call(x.reshape(n, d), ln_in_g, ln_in_b)
    af = filter_stage1(0, zfeat)
    for l in range(depth):
        proj = _mm_call(xb, w_proj[l], BF, 2048, tn_proj, "in_proj")

        hv, hx1, hx2 = _shortconv_call(proj, hy_conv_w[l], hy_conv_b[l], batch, seq, c_hy // hy_cols)
        wide = (2, n1 // 2, DFT_N2 * width)
        y_hy = hv.reshape(wide)
        for o, gx in enumerate((hx1, hx2)):
            y_hy = _hyena_conv(y_hy, gx.reshape(wide), af[o], hy_skip[l, o], cst, seq, width)
        y_hy = y_hy.reshape(n, width)

        y_sw = _attn_call(proj, c_q // q_cols, c_k // kw, q_cols, kw, proj[:, c_v:c_v + kw].T, cos_t, sin_t, swap,
                          sw_sink[l], batch, seq)

        memkv = _mm_call(mem_bf, mem_w_kv[l].astype(BF), BF, mem_tokens, 2 * mq_cols, "mem_kv")
        y_mem = _memattn_call(proj, c_mq // mq_cols, memkv, batch, seq, mem_tokens)

        xf, xb = _merge_call(xf, proj, 0, b_gate[l], y_hy, y_sw, y_mem, w_br_hy[l].astype(BF),
                             w_br_swa[l].astype(BF), w_br_mem[l].astype(BF), w_out[l].astype(BF),
                             ln1_g[l], ln1_b[l], alpha)

        gate, dest, tok_sorted, steps = _route(xf, router_w[l], router_b[l], n_experts)
        if l + 1 < depth:
            anchor = jnp.minimum(tok_sorted[0], 0).astype(F32)
            af = filter_stage1(l + 1, zfeat + anchor)
        xs = xb.at[tok_sorted].get(mode="promise_in_bounds")
        ys = _expert_call(steps, xs, moe_w1, b1g, b1l, moe_w2, b2, perm, l)
        yg = ys.at[dest].get(mode="promise_in_bounds")
        xf, xb = _combine_ln_call(xf, yg, gate, ln2_g[l], ln2_b[l], alpha)
    return xf.reshape(batch, seq, d)
```
